```python
import functools
import jax, jax.numpy as jnp
from jax import lax
import numpy as np

D_MODEL = 1024
BATCH = 8
SEQ = 2048
DEPTH = 4
DEC_BATCH = 128
DEC_SEQ = 8
PAST_LEN = 2048
PAGE_SIZE = 128

N_META = 16
HG_HEADS = 4
HG_DK = 128
HG_DV = 128
HG_WIDTH = HG_HEADS * HG_DK
HG_CHUNK = 64
FOX_HEADS = 8
FOX_DH = 64
FOX_WIDTH = FOX_HEADS * FOX_DH
Q_BLOCK = 128
D_FF = 2816
CONV_W = 3
N_IN = 4 * HG_WIDTH + 3 * FOX_WIDTH + FOX_HEADS + 2 * D_MODEL
EPS = 1e-6
F32 = jnp.float32

kernel_name = "hgrn2_fox_gated_hybrid_step"


def _rmsnorm(x, g):
    xf = x.astype(F32)
    y = xf * lax.rsqrt(jnp.mean(xf * xf, axis=-1, keepdims=True) + EPS)
    return (y * g.astype(F32)).astype(x.dtype)


def _split_in(z):
    sizes = (HG_WIDTH,) * 4 + (FOX_WIDTH,) * 3 + (FOX_HEADS, D_MODEL, D_MODEL)
    parts, start = [], 0
    for s in sizes:
        parts.append(z[..., start:start + s])
        start += s
    return parts


def _hgrn_inputs(zq, zf, zi, lb):
    B, T = zq.shape[:2]
    shp = (B, T, HG_HEADS, HG_DK)
    q = jax.nn.silu(zq.astype(F32)).reshape(shp)
    zf = zf.astype(F32).reshape(shp)
    lb = lb.reshape(HG_HEADS, HG_DK)
    logf = jnp.logaddexp(jnp.log(lb), jnp.log1p(-lb) + jax.nn.log_sigmoid(zf))
    k = (1.0 - lb) * jax.nn.sigmoid(-zf)
    v = zi.astype(F32).reshape(B, T, HG_HEADS, HG_DV)
    return q, k, v, logf


def _hgrn_chunk(S0, q, k, v, logf):
    C = q.shape[1]
    b = jnp.cumsum(logf, axis=1)
    o_inter = jnp.einsum("bthk,bhkv->bthv", q * jnp.exp(b), S0)
    causal = jnp.tril(jnp.ones((C, C), bool))[None, :, :, None, None]
    diff = b[:, :, None] - b[:, None, :]
    decay = jnp.exp(jnp.where(causal, diff, -jnp.inf))
    scores = jnp.sum(q[:, :, None] * k[:, None, :] * decay, axis=-1)
    o_intra = jnp.einsum("btsh,bshv->bthv", scores, v)
    b_end = b[:, -1]
    S_new = jnp.exp(b_end)[..., None] * S0 + jnp.einsum(
        "bshk,bshv->bhkv", k * jnp.exp(b_end[:, None] - b), v)
    return S_new, o_inter + o_intra


def _hgrn_prompt(q, k, v, logf):
    B, T = q.shape[:2]
    n_chunks = (T - N_META) // HG_CHUNK
    S0 = jnp.zeros((B, HG_HEADS, HG_DK, HG_DV), F32)
    S_meta, o_meta = _hgrn_chunk(S0, q[:, :N_META], k[:, :N_META], v[:, :N_META], logf[:, :N_META])

    def chunks(a):
        return jnp.swapaxes(a[:, N_META:].reshape((B, n_chunks, HG_CHUNK) + a.shape[2:]), 0, 1)

    def step(S, xs):
        return _hgrn_chunk(S, *xs)

    S_fin, o_ch = lax.scan(step, S_meta, (chunks(q), chunks(k), chunks(v), chunks(logf)))
    o_real = jnp.swapaxes(o_ch, 0, 1).reshape(B, T - N_META, HG_HEADS, HG_DV)
    return jnp.concatenate([o_meta, o_real], axis=1), S_fin


def _hgrn_sample(S0, q, k, v, logf):
    S_new, o = _hgrn_chunk(S0.astype(F32), q, k, v, logf)
    return o, S_new


def _hgrn_out(o, zg, gain):
    B, T = o.shape[:2]
    o = o * lax.rsqrt(jnp.mean(o * o, axis=-1, keepdims=True) + EPS)
    o = o.reshape(B, T, HG_WIDTH) * gain.astype(F32) * jax.nn.silu(zg.astype(F32))
    return o.astype(zg.dtype)


def _fox_attend(q, k, v, c_q, c_k, q_pos, k_pos):
    s = jnp.einsum("bthd,bshd->bhts", q, k) * (FOX_DH ** -0.5)
    s = s + jnp.swapaxes(c_q, 1, 2)[..., None] - jnp.swapaxes(c_k, 1, 2)[:, :, None, :]
    s = jnp.where((k_pos[None, :] <= q_pos[:, None])[None, None], s, -jnp.inf)
    p = jax.nn.softmax(s, axis=-1)
    return jnp.einsum("bhts,bshd->bthd", p, v)


def _fox_prompt(q, k, v, logf):
    B, T = q.shape[:2]
    qf, kf, vf = q.astype(F32), k.astype(F32), v.astype(F32)
    c = jnp.cumsum(logf, axis=1)
    k_pos = jnp.arange(T)
    o_meta = _fox_attend(qf[:, :N_META], kf, vf, c[:, :N_META], c, jnp.arange(N_META), k_pos)
    n_blocks = (T - N_META) // Q_BLOCK
    q_blk = jnp.swapaxes(qf[:, N_META:].reshape(B, n_blocks, Q_BLOCK, FOX_HEADS, FOX_DH), 0, 1)
    c_blk = jnp.swapaxes(c[:, N_META:].reshape(B, n_blocks, Q_BLOCK, FOX_HEADS), 0, 1)
    p_blk = (N_META + jnp.arange(T - N_META)).reshape(n_blocks, Q_BLOCK)
    o_blk = lax.map(lambda a: _fox_attend(a[0], kf, vf, a[1], c, a[2], k_pos), (q_blk, c_blk, p_blk))
    o_real = jnp.swapaxes(o_blk, 0, 1).reshape(B, T - N_META, FOX_HEADS, FOX_DH)
    o = jnp.concatenate([o_meta, o_real], axis=1)
    return o.reshape(B, T, FOX_WIDTH).astype(v.dtype)


def _fox_sample(k_past, v_past, logf_past, q, k, v, logf):
    B, T = q.shape[:2]
    P = k_past.shape[1] * k_past.shape[2]
    kf = jnp.concatenate([k_past.reshape(B, P, FOX_HEADS, FOX_DH).astype(F32), k.astype(F32)], axis=1)
    vf = jnp.concatenate([v_past.reshape(B, P, FOX_HEADS, FOX_DH).astype(F32), v.astype(F32)], axis=1)
    c = jnp.cumsum(jnp.concatenate([logf_past.reshape(B, P, FOX_HEADS).astype(F32), logf], axis=1), axis=1)
    o = _fox_attend(q.astype(F32), kf, vf, c[:, P:], c, P + jnp.arange(T), jnp.arange(P + T))
    return o.reshape(B, T, FOX_WIDTH).astype(v.dtype)


def _token_mix(h, w_in_l, b_in_l, lb_l, hg_norm_l, w_oa_l, w_ob_l, w_o_l, hgrn_fn, fox_fn):
    B, T, _ = h.shape
    z = h @ w_in_l + b_in_l
    zq, zf, zi, zg, fq, fk, fv, ff, ga, gb = _split_in(z)
    q, k, v, logf = _hgrn_inputs(zq, zf, zi, lb_l)
    o_hg, s_new = hgrn_fn(q, k, v, logf)
    y_a = _hgrn_out(o_hg, zg, hg_norm_l) @ w_oa_l
    fq = fq.reshape(B, T, FOX_HEADS, FOX_DH)
    fk = fk.reshape(B, T, FOX_HEADS, FOX_DH)
    fv = fv.reshape(B, T, FOX_HEADS, FOX_DH)
    fox_logf = jax.nn.log_sigmoid(ff.astype(F32))
    y_b = fox_fn(fq, fk, fv, fox_logf) @ w_ob_l
    merged = jax.nn.sigmoid(ga) * y_a + jax.nn.sigmoid(gb) * y_b
    return merged @ w_o_l, s_new, fk, fv, fox_logf


def _conv_ffn(h, conv_state, w_up_l, conv_w_l, conv_b_l, w_down_l):
    T = h.shape[1]
    a, g = jnp.split(h @ w_up_l, 2, axis=-1)
    a_ext = jnp.concatenate([conv_state.astype(a.dtype), a], axis=1)
    conv = conv_b_l + conv_w_l[0] * a_ext[:, 0:T]
    for j in range(1, CONV_W):
        conv = conv + conv_w_l[j] * a_ext[:, j:j + T]
    y = (jax.nn.silu(conv) * g) @ w_down_l
    return y, a_ext[:, T:]


def setup_inputs(seed: int = 0) -> dict:
    key = jax.random.key(seed)
    ks = jax.random.split(key, 24)
    n_pages = PAST_LEN // PAGE_SIZE
    n_used = DEC_BATCH * n_pages
    n_pool = n_used + (n_used + 3) // 4

    def nrm(k, shape, scale=1.0):
        return scale * jax.random.normal(k, shape, F32)

    page_table = jax.random.permutation(ks[0], n_pool)[:n_used].reshape(DEC_BATCH, n_pages).astype(jnp.int32)
    return {
        "x_prompt": nrm(ks[1], (BATCH, SEQ, D_MODEL)),
        "x_sample": nrm(ks[2], (DEC_BATCH, DEC_SEQ, D_MODEL)),
        "cache_k": nrm(ks[3], (DEPTH, n_pool, PAGE_SIZE, FOX_HEADS, FOX_DH)),
        "cache_v": nrm(ks[4], (DEPTH, n_pool, PAGE_SIZE, FOX_HEADS, FOX_DH)),
        "cache_logf": jax.nn.log_sigmoid(nrm(ks[5], (DEPTH, n_pool, PAGE_SIZE, FOX_HEADS))),
        "state_hgrn": nrm(ks[6], (DEPTH, DEC_BATCH, HG_HEADS, HG_DK, HG_DV)),
        "state_conv": nrm(ks[7], (DEPTH, DEC_BATCH, CONV_W - 1, D_FF)),
        "page_table": page_table,
        "meta_tokens": nrm(ks[8], (N_META, D_MODEL)),
        "norm1": 1.0 + nrm(ks[9], (DEPTH, D_MODEL), 0.01),
        "norm2": 1.0 + nrm(ks[10], (DEPTH, D_MODEL), 0.01),
        "norm_f": 1.0 + nrm(ks[11], (D_MODEL,), 0.01),
        "w_in": nrm(ks[12], (DEPTH, D_MODEL, N_IN), D_MODEL ** -0.5),
        "b_in": nrm(ks[13], (DEPTH, N_IN), 0.01),
        "hg_lower_bounds": nrm(ks[14], (DEPTH, HG_WIDTH), 0.1),
        "hg_norm": 1.0 + nrm(ks[15], (DEPTH, HG_WIDTH), 0.01),
        "w_oa": nrm(ks[16], (DEPTH, HG_WIDTH, D_MODEL), HG_WIDTH ** -0.5),
        "w_ob": nrm(ks[17], (DEPTH, FOX_WIDTH, D_MODEL), FOX_WIDTH ** -0.5),
        "w_o": nrm(ks[18], (DEPTH, D_MODEL, D_MODEL), D_MODEL ** -0.5),
        "w_up": nrm(ks[19], (DEPTH, D_MODEL, 2 * D_FF), D_MODEL ** -0.5),
        "conv_w": nrm(ks[20], (DEPTH, CONV_W, D_FF), CONV_W ** -0.5),
        "conv_b": nrm(ks[21], (DEPTH, D_FF), 0.01),
        "w_down": nrm(ks[22], (DEPTH, D_FF, D_MODEL), D_FF ** -0.5),
    }


def reference(x_prompt, x_sample, cache_k, cache_v, cache_logf, state_hgrn, state_conv, page_table,
              meta_tokens, norm1, norm2, norm_f, w_in, b_in, hg_lower_bounds, hg_norm,
              w_oa, w_ob, w_o, w_up, conv_w, conv_b, w_down):
    B = x_prompt.shape[0]
    lb_all = jnp.cumsum(jax.nn.softmax(hg_lower_bounds.astype(F32), axis=0), axis=0)
    lb_all = lb_all - lb_all[0]
    meta = jnp.broadcast_to(meta_tokens.astype(x_prompt.dtype)[None], (B, N_META, D_MODEL))
    xp = jnp.concatenate([meta, x_prompt], axis=1)
    xs = x_sample
    conv0 = jnp.zeros((B, CONV_W - 1, D_FF), x_prompt.dtype)
    kp, vp, lp, hgp, cvp = [], [], [], [], []
    ksm, vsm, lsm, hgs, cvs = [], [], [], [], []
    for l in range(DEPTH):
        mix_w = (w_in[l], b_in[l], lb_all[l], hg_norm[l], w_oa[l], w_ob[l], w_o[l])
        mix, s_new, fk, fv, flf = _token_mix(_rmsnorm(xp, norm1[l]), *mix_w, _hgrn_prompt, _fox_prompt)
        xp = xp + mix
        f, c_new = _conv_ffn(_rmsnorm(xp, norm2[l]), conv0, w_up[l], conv_w[l], conv_b[l], w_down[l])
        xp = xp + f
        kp.append(fk); vp.append(fv); lp.append(flf); hgp.append(s_new); cvp.append(c_new)
        hg_fn = functools.partial(_hgrn_sample, state_hgrn[l])
        fox_fn = functools.partial(_fox_sample, cache_k[l, page_table], cache_v[l, page_table],
                                   cache_logf[l, page_table])
        mix, s_new, fk, fv, flf = _token_mix(_rmsnorm(xs, norm1[l]), *mix_w, hg_fn, fox_fn)
        xs = xs + mix
        f, c_new = _conv_ffn(_rmsnorm(xs, norm2[l]), state_conv[l], w_up[l], conv_w[l], conv_b[l], w_down[l])
        xs = xs + f
        ksm.append(fk); vsm.append(fv); lsm.append(flf); hgs.append(s_new); cvs.append(c_new)
    y_prompt = _rmsnorm(xp, norm_f)[:, N_META:]
    y_sample = _rmsnorm(xs, norm_f)
    return (y_prompt, y_sample,
            jnp.stack(kp), jnp.stack(vp), jnp.stack(lp), jnp.stack(hgp), jnp.stack(cvp),
            jnp.stack(ksm), jnp.stack(vsm), jnp.stack(lsm), jnp.stack(hgs), jnp.stack(cvs))
```

```python
import functools

import jax
import jax.numpy as jnp
from jax import lax
from jax.experimental import pallas as pl
from jax.experimental.pallas import tpu as pltpu

F32 = jnp.float32
BF16 = jnp.bfloat16
EPS = 1e-6
NEG_INF = float("-inf")

N_META = 16
HG_HEADS = 4
HG_DK = 128
FOX_HEADS = 8
FOX_DH = 64
CONV_W = 3

LANES = 128
HG_SUB = 16
VMEM_LIMIT = 56 * 1024 * 1024

HG_W = HG_HEADS * HG_DK
FOX_W = FOX_HEADS * FOX_DH
C_GA = 0
C_GB = 1024
C_ZQ = 2048
C_ZF = C_ZQ + HG_W
C_ZI = C_ZF + HG_W
C_ZG = C_ZI + HG_W
C_FQ = C_ZG + HG_W
C_FK = C_FQ + FOX_W
C_FV = C_FK + FOX_W
C_FF = C_FV + FOX_W
NZ = C_FF + LANES


def _cparams(*sem):
    return pltpu.CompilerParams(dimension_semantics=sem, vmem_limit_bytes=VMEM_LIMIT)


def _pick_tile(n, cap):
    best = None
    for d in range(8, min(n, cap) + 1, 8):
        if n % d == 0:
            best = d
    assert best is not None, (n, cap)
    return best


def _sigmoid(x):
    return 1.0 / (1.0 + jnp.exp(-x))


def _log_sigmoid(x):
    return jnp.minimum(x, 0.0) - jnp.log1p(jnp.exp(-jnp.abs(x)))


def _split3(x):
    hi = x.astype(BF16)
    r = x - hi.astype(F32)
    mid = r.astype(BF16)
    r = r - mid.astype(F32)
    return hi, mid, r.astype(BF16)


def _dot_f32(a, b, dims=None):
    if dims is None:
        return jnp.dot(a, b, preferred_element_type=F32)
    return lax.dot_general(a, b, (dims, ((), ())), preferred_element_type=F32)


NT = ((1,), (1,))


def _mask_dot_left(m, x):
    return sum(_dot_f32(m, p) for p in _split3(x))


def _mask_dot_right(x, m):
    return sum(_dot_f32(p, m) for p in _split3(x))


def _rmsnorm(x, g):
    return x * lax.rsqrt(jnp.mean(x * x, axis=-1, keepdims=True) + EPS) * g


def _iota(shape, dim):
    return lax.broadcasted_iota(jnp.int32, shape, dim)


def _inproj_kernel(x_ref, g_ref, w_ref, b_ref, z_ref, h_scr):
    @pl.when(pl.program_id(1) == 0)
    def _():
        h_scr[...] = _rmsnorm(x_ref[...], g_ref[...]).astype(BF16)

    z_ref[...] = _dot_f32(h_scr[...], w_ref[...]) + b_ref[...]


def _inproj(x, g, w, b):
    n, d = x.shape
    tm = _pick_tile(n, 1040)
    tn = 1152
    assert NZ % tn == 0
    return pl.pallas_call(
        _inproj_kernel,
        grid=(n // tm, NZ // tn),
        in_specs=[
            pl.BlockSpec((tm, d), lambda i, j: (i, 0)),
            pl.BlockSpec((1, d), lambda i, j: (0, 0)),
            pl.BlockSpec((d, tn), lambda i, j: (0, j)),
            pl.BlockSpec((1, tn), lambda i, j: (0, j)),
        ],
        out_specs=pl.BlockSpec((tm, tn), lambda i, j: (i, j)),
        out_shape=jax.ShapeDtypeStruct((n, NZ), F32),
        scratch_shapes=[pltpu.VMEM((tm, d), BF16)],
        compiler_params=_cparams("parallel", "arbitrary"),
        name="inproj",
    )(x, g.reshape(1, d), w, b.reshape(1, NZ))


def _hgrn_lower_bound(lbraw, layer):
    e = jnp.exp(lbraw - jnp.max(lbraw, axis=0, keepdims=True))
    sm = e / jnp.sum(e, axis=0, keepdims=True)
    lb = jnp.zeros_like(sm[0:1])
    for i in range(1, layer + 1):
        lb = lb + sm[i:i + 1]
    return lb


def _hgrn_gates(zq, zf, zi, lb):
    q = zq * _sigmoid(zq)
    a = jnp.log(lb)
    b = jnp.log1p(-lb) + _log_sigmoid(zf)
    logf = jnp.maximum(a, b) + jnp.log1p(jnp.exp(-jnp.abs(a - b)))
    k = (1.0 - lb) / (1.0 + jnp.exp(zf))
    return q, k, zi, logf


def _hgrn_diag(q, k, b, v):
    n = q.shape[0]
    rowid = _iota(q.shape, 0)
    acc = jnp.zeros_like(v)
    for s in range(n):
        d = jnp.where(rowid >= s, b - b[s:s + 1], NEG_INF)
        w = q * k[s:s + 1] * jnp.exp(d)
        acc = acc + jnp.sum(w, axis=-1, keepdims=True) * v[s:s + 1]
    return acc


def _hgrn_head_out(o, zg, gain):
    o = o * lax.rsqrt(jnp.mean(o * o, axis=-1, keepdims=True) + EPS)
    return o * gain * (zg * _sigmoid(zg))


def _pad_rows(x, rows):
    if x.shape[0] == rows:
        return x
    return jnp.concatenate([x, jnp.zeros((rows - x.shape[0], x.shape[1]), x.dtype)], axis=0)


def _hgrn_prompt_kernel(zq_ref, zf_ref, zi_ref, zg_ref, lbraw_ref, gain_ref, y_ref, s_ref, st_scr,
                        *, layer, seq):
    ck = LANES
    n_full, tail = divmod(seq, ck)
    assert tail % HG_SUB == 0
    lb = _hgrn_lower_bound(lbraw_ref[...], layer)
    gain = gain_ref[...]
    row = _iota((ck, ck), 0)
    col = _iota((ck, ck), 1)
    sub_of_col = col // HG_SUB
    ltri = jnp.where((row // HG_SUB == sub_of_col) & (col <= row), 1.0, 0.0).astype(BF16)
    st_scr[...] = jnp.zeros_like(st_scr)

    def chunk(r0, n_rows):
        sl = pl.ds(r0, n_rows)
        q, k, v, logf = _hgrn_gates(zq_ref[0, sl, :], zf_ref[0, sl, :], zi_ref[0, sl, :], lb)
        q, k, v, logf = (_pad_rows(a, ck) for a in (q, k, v, logf))
        n_sub = n_rows // HG_SUB
        b = _mask_dot_left(ltri, logf)
        bend = jnp.concatenate(
            [jnp.broadcast_to(b[HG_SUB * i + HG_SUB - 1:HG_SUB * (i + 1)], (HG_SUB, ck))
             for i in range(ck // HG_SUB)], axis=0)
        qt = (q * jnp.exp(b)).astype(BF16)
        kt = (k * jnp.exp(bend - b)).astype(BF16)
        vt = v.T
        lhs = jnp.concatenate([jnp.where(sub_of_col == i, vt, 0.0) for i in range(n_sub)], axis=0)
        pall = _dot_f32(lhs.astype(BF16), kt)
        for i in range(n_sub):
            rs = slice(HG_SUB * i, HG_SUB * (i + 1))
            st = st_scr[...]
            o = _dot_f32(qt[rs], st.astype(BF16), NT) + _hgrn_diag(q[rs], k[rs], b[rs], v[rs])
            st_scr[...] = st * jnp.exp(b[HG_SUB * (i + 1) - 1:HG_SUB * (i + 1)]) + pall[ck * i:ck * (i + 1)]
            osl = pl.ds(r0 + HG_SUB * i, HG_SUB)
            y_ref[0, osl, :] = _hgrn_head_out(o, zg_ref[0, osl, :], gain).astype(y_ref.dtype)

    def body(c, carry):
        chunk(pl.multiple_of(c * ck, ck), ck)
        return carry

    lax.fori_loop(0, n_full, body, 0)
    if tail:
        chunk(n_full * ck, tail)
    s_ref[0, 0] = st_scr[...].T


def _hgrn_prompt(z3, lbraw, gain, layer):
    nb, seq, _ = z3.shape

    def zspec(c0):
        return pl.BlockSpec((1, seq, LANES), lambda b, h, c0=c0: (b, 0, c0 // LANES + h))

    return pl.pallas_call(
        functools.partial(_hgrn_prompt_kernel, layer=layer, seq=seq),
        grid=(nb, HG_HEADS),
        in_specs=[zspec(C_ZQ), zspec(C_ZF), zspec(C_ZI), zspec(C_ZG),
                  pl.BlockSpec((lbraw.shape[0], LANES), lambda b, h: (0, h)),
                  pl.BlockSpec((1, LANES), lambda b, h: (0, h))],
        out_specs=[pl.BlockSpec((1, seq, LANES), lambda b, h: (b, 0, h)),
                   pl.BlockSpec((1, 1, HG_DK, HG_DK), lambda b, h: (b, h, 0, 0))],
        out_shape=[jax.ShapeDtypeStruct((nb, seq, HG_W), BF16),
                   jax.ShapeDtypeStruct((nb, HG_HEADS, HG_DK, HG_DK), F32)],
        scratch_shapes=[pltpu.VMEM((HG_DK, HG_DK), F32)],
        compiler_params=_cparams("parallel", "parallel"),
        name="hgrn_prompt",
    )(z3, z3, z3, z3, lbraw, gain.reshape(1, HG_W))


def _hgrn_sample_kernel(zq_ref, zf_ref, zi_ref, zg_ref, lbraw_ref, gain_ref, s0_ref, y_ref, s_ref,
                        *, layer, seq):
    rows = zq_ref.shape[0]
    n_seq = rows // seq
    lb = _hgrn_lower_bound(lbraw_ref[...], layer)
    q, k, v, logf = _hgrn_gates(zq_ref[...], zf_ref[...], zi_ref[...], lb)
    row = _iota((rows, rows), 0)
    col = _iota((rows, rows), 1)
    ltri = jnp.where((row // seq == col // seq) & (col <= row), 1.0, 0.0).astype(BF16)
    b = _mask_dot_left(ltri, logf)
    bend = jnp.concatenate(
        [jnp.broadcast_to(b[seq * s + seq - 1:seq * (s + 1)], (seq, LANES)) for s in range(n_seq)], axis=0)
    qt = (q * jnp.exp(b)).astype(BF16)
    kt = k * jnp.exp(bend - b)
    outs = []
    for s in range(n_seq):
        rs = slice(seq * s, seq * (s + 1))
        s0 = s0_ref[s, 0]
        outs.append(_dot_f32(qt[rs], s0.astype(BF16)) + _hgrn_diag(q[rs], k[rs], b[rs], v[rs]))
        x = jnp.concatenate([kt[rs], jnp.exp(bend[rs]), jnp.zeros((HG_DK - 2 * seq, LANES), F32)], axis=0)
        xt = x.T
        inc = _dot_f32(xt.astype(BF16), _pad_rows(v[rs], HG_DK).astype(BF16))
        s_ref[s, 0] = xt[:, seq:seq + 1] * s0 + inc
    o = jnp.concatenate(outs, axis=0)
    y_ref[...] = _hgrn_head_out(o, zg_ref[...], gain_ref[...]).astype(y_ref.dtype)


def _hgrn_sample(z, state, lbraw, gain, layer, seq):
    n = z.shape[0]
    rows = LANES
    assert n % rows == 0 and rows % seq == 0 and 2 * seq <= HG_DK
    sblk = rows // seq

    def zspec(c0):
        return pl.BlockSpec((rows, LANES), lambda i, h, c0=c0: (i, c0 // LANES + h))

    st_spec = pl.BlockSpec((sblk, 1, HG_DK, HG_DK), lambda i, h: (i, h, 0, 0))
    return pl.pallas_call(
        functools.partial(_hgrn_sample_kernel, layer=layer, seq=seq),
        grid=(n // rows, HG_HEADS),
        in_specs=[zspec(C_ZQ), zspec(C_ZF), zspec(C_ZI), zspec(C_ZG),
                  pl.BlockSpec((lbraw.shape[0], LANES), lambda i, h: (0, h)),
                  pl.BlockSpec((1, LANES), lambda i, h: (0, h)),
                  st_spec],
        out_specs=[pl.BlockSpec((rows, LANES), lambda i, h: (i, h)), st_spec],
        out_shape=[jax.ShapeDtypeStruct((n, HG_W), BF16),
                   jax.ShapeDtypeStruct(state.shape, F32)],
        compiler_params=_cparams("parallel", "parallel"),
        name="hgrn_sample",
    )(z, z, z, z, lbraw, gain.reshape(1, HG_W), state)


def _fox_gate_kernel(ff_ref, lf_ref, c_ref, *, seq):
    ck = LANES
    n_full, tail = divmod(seq, ck)
    ltri = jnp.where(_iota((ck, ck), 1) <= _iota((ck, ck), 0), 1.0, 0.0).astype(BF16)

    def chunk(r0, n_rows, carry):
        sl = pl.ds(r0, n_rows)
        lf = _log_sigmoid(ff_ref[0, sl, :])
        lf_ref[0, sl, :] = lf
        c = _mask_dot_left(ltri, _pad_rows(lf, ck)) + carry
        c_ref[0, sl, :] = c[:n_rows]
        return c[n_rows - 1:n_rows]

    carry = lax.fori_loop(0, n_full, lambda i, cr: chunk(pl.multiple_of(i * ck, ck), ck, cr),
                          jnp.zeros((1, LANES), F32))
    if tail:
        chunk(n_full * ck, tail, carry)


def _fox_gate(z3):
    nb, seq, _ = z3.shape
    spec = pl.BlockSpec((1, seq, LANES), lambda b: (b, 0, 0))
    return pl.pallas_call(
        functools.partial(_fox_gate_kernel, seq=seq),
        grid=(nb,),
        in_specs=[pl.BlockSpec((1, seq, LANES), lambda b: (b, 0, C_FF // LANES))],
        out_specs=[spec, spec],
        out_shape=[jax.ShapeDtypeStruct((nb, seq, LANES), F32)] * 2,
        compiler_params=_cparams("parallel"),
        name="fox_gate",
    )(z3)


def _fox_attn_kernel(fq_ref, fk_ref, fv_ref, c_ref, o_ref, qa_scr, ka_scr, vb_scr, *, seq, bq):
    pair = pl.program_id(1)
    padded = qa_scr.shape[1]
    nq = padded // bq
    ck = LANES
    n_full, tail = divmod(seq, ck)
    scale = FOX_DH ** -0.5

    def build(r0, n_rows):
        sl = pl.ds(r0, n_rows)
        qf, kf, cb = fq_ref[0, sl, :], fk_ref[0, sl, :], c_ref[0, sl, :]
        vb_scr[sl, :] = fv_ref[0, sl, :].astype(BF16)
        lane = _iota((n_rows, LANES), 1)
        for hh in range(2):
            ch = jnp.sum(jnp.where(lane == 2 * pair + hh, cb, 0.0), axis=-1, keepdims=True)
            hi, mid, lo = (p.astype(F32) for p in _split3(ch))
            own = (lane >= FOX_DH * hh) & (lane < FOX_DH * (hh + 1))
            f0 = FOX_DH * (1 - hh)
            ones_q = (lane >= f0 + 3) & (lane < f0 + 6)
            ones_k = (lane >= f0) & (lane < f0 + 3)
            qa = jnp.where(own, qf * scale,
                           jnp.where(lane == f0, hi, jnp.where(lane == f0 + 1, mid, jnp.where(
                               lane == f0 + 2, lo, jnp.where(ones_q, 1.0, 0.0)))))
            ka = jnp.where(own, kf,
                           jnp.where(ones_k, 1.0, jnp.where(lane == f0 + 3, -hi, jnp.where(
                               lane == f0 + 4, -mid, jnp.where(lane == f0 + 5, -lo, 0.0)))))
            qa_scr[hh, sl, :] = qa.astype(BF16)
            ka_scr[hh, sl, :] = ka.astype(BF16)

    def build_body(i, carry):
        build(pl.multiple_of(i * ck, ck), ck)
        return carry

    lax.fori_loop(0, n_full, build_body, 0)
    if tail:
        build(n_full * ck, tail)
    if padded > seq:
        zpad = jnp.zeros((padded - seq, LANES), BF16)
        vb_scr[seq:padded, :] = zpad
        for hh in range(2):
            qa_scr[hh, seq:padded, :] = zpad
            ka_scr[hh, seq:padded, :] = zpad

    causal = _iota((bq, bq), 1) <= _iota((bq, bq), 0)
    lane_q = _iota((bq, LANES), 1)

    def qblock(qi, n_store):
        q0 = qi * bq if isinstance(qi, int) else pl.multiple_of(qi * bq, bq)
        outs = []
        for hh in range(2):
            qa = qa_scr[hh, pl.ds(q0, bq), :]

            def update(carry, k0, mask):
                m, l, acc = carry
                s = _dot_f32(qa, ka_scr[hh, pl.ds(k0, bq), :], NT)
                if mask:
                    s = jnp.where(causal, s, NEG_INF)
                m_new = jnp.maximum(m, jnp.max(s, axis=-1, keepdims=True))
                p = jnp.exp(s - m_new)
                alpha = jnp.exp(m - m_new)
                l = alpha * l + jnp.sum(p, axis=-1, keepdims=True)
                acc = alpha * acc + _dot_f32(p.astype(BF16), vb_scr[pl.ds(k0, bq), :])
                return m_new, l, acc

            init = (jnp.full((bq, 1), NEG_INF, F32), jnp.zeros((bq, 1), F32), jnp.zeros((bq, LANES), F32))
            carry = lax.fori_loop(
                0, qi, lambda j, cr: update(cr, pl.multiple_of(j * bq, bq), False), init)
            _, l, acc = update(carry, q0, True)
            outs.append(acc / l)
        o = jnp.where(lane_q < FOX_DH, outs[0], outs[1])
        o_ref[0, pl.ds(q0, n_store), :] = o[:n_store].astype(o_ref.dtype)

    def q_body(qi, carry):
        qblock(qi, bq)
        return carry

    n_whole = seq // bq
    lax.fori_loop(0, n_whole, q_body, 0)
    if n_whole < nq:
        qblock(n_whole, seq - n_whole * bq)


def _fox_attn(z3, c):
    nb, seq, _ = z3.shape
    bq = 256
    padded = -(-seq // bq) * bq

    def zspec(c0):
        return pl.BlockSpec((1, seq, LANES), lambda b, p, c0=c0: (b, 0, c0 // LANES + p))

    return pl.pallas_call(
        functools.partial(_fox_attn_kernel, seq=seq, bq=bq),
        grid=(nb, FOX_HEADS // 2),
        in_specs=[zspec(C_FQ), zspec(C_FK), zspec(C_FV),
                  pl.BlockSpec((1, seq, LANES), lambda b, p: (b, 0, 0))],
        out_specs=pl.BlockSpec((1, seq, LANES), lambda b, p: (b, 0, p)),
        out_shape=jax.ShapeDtypeStruct((nb, seq, FOX_W), BF16),
        scratch_shapes=[pltpu.VMEM((2, padded, LANES), BF16), pltpu.VMEM((2, padded, LANES), BF16),
                        pltpu.VMEM((padded, LANES), BF16)],
        compiler_params=_cparams("parallel", "parallel"),
        name="fox_attn",
    )(z3, z3, z3, c)


def _fox_decode_kernel(pt_ref, fq_ref, fk_ref, fv_ref, ff_ref, *rest, n_pages, page, seq):
    del pt_ref
    k_refs, v_refs, lf_refs = rest[:n_pages], rest[n_pages:2 * n_pages], rest[2 * n_pages:3 * n_pages]
    o_ref, lfo_ref = rest[3 * n_pages:]
    assert page == LANES
    lfn = _log_sigmoid(ff_ref[...])
    lfo_ref[...] = lfn
    lfn = jnp.where(_iota(lfn.shape, 1) < FOX_HEADS, lfn, 0.0)
    lfn_t = _pad_rows(lfn, LANES).T[:FOX_HEADS]
    stack = jnp.concatenate([r[0, 0] for r in lf_refs] + [lfn_t], axis=0)
    ustrict = jnp.where(_iota((LANES, LANES), 0) > _iota((LANES, LANES), 1), 1.0, 0.0).astype(BF16)
    within = _mask_dot_right(stack, ustrict)
    tot = jnp.sum(stack, axis=-1, keepdims=True)
    g = [None] * (n_pages + 1)
    run = jnp.zeros((FOX_HEADS, 1), F32)
    for p in reversed(range(n_pages + 1)):
        rs = slice(FOX_HEADS * p, FOX_HEADS * (p + 1))
        g[p] = within[rs] + run
        run = run + tot[rs]
    gq = _pad_rows(g[n_pages], LANES).T[:seq]
    lane = _iota((seq, LANES), 1)
    tok = _iota((seq, LANES), 0)
    q = fq_ref[...] * (FOX_DH ** -0.5)
    knew, vnew = fk_ref[...], fv_ref[...]
    for h in range(FOX_HEADS):
        hs = slice(FOX_DH * h, FOX_DH * (h + 1))
        qh = q[:, hs].astype(BF16)
        gq_h = jnp.sum(jnp.where(lane == h, gq, 0.0), axis=-1, keepdims=True)
        s_blocks = []
        for p in range(n_pages):
            s_blocks.append(_dot_f32(qh, k_refs[p][0, 0, h].astype(BF16)) + (g[p][h:h + 1] - gq_h))
        s = _dot_f32(qh, _pad_rows(knew[:, hs], page).astype(BF16), NT) + (g[n_pages][h:h + 1] - gq_h)
        s_blocks.append(jnp.where(lane <= tok, s, NEG_INF))
        mx = s_blocks[0]
        for s in s_blocks[1:]:
            mx = jnp.maximum(mx, s)
        m = jnp.max(mx, axis=-1, keepdims=True)
        l = jnp.zeros((seq, 1), F32)
        acc = jnp.zeros((seq, FOX_DH), F32)
        for p in range(n_pages + 1):
            pe = jnp.exp(s_blocks[p] - m)
            l = l + jnp.sum(pe, axis=-1, keepdims=True)
            if p < n_pages:
                acc = acc + _dot_f32(pe.astype(BF16), v_refs[p][0, 0, h].astype(BF16), NT)
            else:
                acc = acc + _dot_f32(pe.astype(BF16), _pad_rows(vnew[:, hs], page).astype(BF16))
        o_ref[0, h] = acc / l


def _fox_decode(z, cache_kt, cache_vt, cache_lft, page_table, layer, seq):
    n = z.shape[0]
    nb, n_pages = page_table.shape
    page = cache_lft.shape[-1]
    assert nb * seq == n and seq == 8

    def zspec(c0, w):
        return pl.BlockSpec((seq, w), lambda b, pt, c0=c0, w=w: (b, c0 // w))

    def pspec(shape, p):
        return pl.BlockSpec(shape, lambda b, pt, p=p: (layer, pt[b, p]) + (0,) * (len(shape) - 2))

    kv_shape = (1, 1, FOX_HEADS, FOX_DH, page)
    lf_shape = (1, 1, FOX_HEADS, page)
    in_specs = [zspec(C_FQ, FOX_W), zspec(C_FK, FOX_W), zspec(C_FV, FOX_W), zspec(C_FF, LANES)]
    in_specs += [pspec(kv_shape, p) for p in range(n_pages)]
    in_specs += [pspec(kv_shape, p) for p in range(n_pages)]
    in_specs += [pspec(lf_shape, p) for p in range(n_pages)]
    grid_spec = pltpu.PrefetchScalarGridSpec(
        num_scalar_prefetch=1,
        grid=(nb,),
        in_specs=in_specs,
        out_specs=[pl.BlockSpec((1, FOX_HEADS, seq, FOX_DH), lambda b, pt: (b, 0, 0, 0)),
                   pl.BlockSpec((seq, LANES), lambda b, pt: (b, 0))],
    )
    return pl.pallas_call(
        functools.partial(_fox_decode_kernel, n_pages=n_pages, page=page, seq=seq),
        grid_spec=grid_spec,
        out_shape=[jax.ShapeDtypeStruct((nb, FOX_HEADS, seq, FOX_DH), F32),
                   jax.ShapeDtypeStruct((n, LANES), F32)],
        compiler_params=_cparams("arbitrary"),
        name="fox_decode",
    )(page_table, z, z, z, z, *([cache_kt] * n_pages), *([cache_vt] * n_pages), *([cache_lft] * n_pages))


def _mix_kernel(x_ref, ya_ref, yb_ref, ga_ref, gb_ref, woa_ref, wob_ref, wo_ref, o_ref):
    ya = _dot_f32(ya_ref[...].astype(BF16), woa_ref[...])
    yb = _dot_f32(yb_ref[...].astype(BF16), wob_ref[...])
    merged = _sigmoid(ga_ref[...]) * ya + _sigmoid(gb_ref[...]) * yb
    o_ref[...] = x_ref[...] + _dot_f32(merged.astype(BF16), wo_ref[...])


def _mix(x, ya, yb, z, woa, wob, wo):
    n, d = x.shape
    assert d == C_GB - C_GA
    tm = _pick_tile(n, 704)
    row = lambda w: pl.BlockSpec((tm, w), lambda i: (i, 0))
    full = lambda a: pl.BlockSpec(a.shape, lambda i: (0, 0))
    return pl.pallas_call(
        _mix_kernel,
        grid=(n // tm,),
        in_specs=[row(d), row(HG_W), row(FOX_W),
                  pl.BlockSpec((tm, d), lambda i: (i, C_GA // d)),
                  pl.BlockSpec((tm, d), lambda i: (i, C_GB // d)),
                  full(woa), full(wob), full(wo)],
        out_specs=row(d),
        out_shape=jax.ShapeDtypeStruct((n, d), F32),
        compiler_params=_cparams("parallel"),
        name="mix_out",
    )(x, ya, yb, z, z, woa, wob, wo)


def _ffn_kernel(x_ref, g_ref, wa_ref, wg_ref, cw_ref, cb_ref, wd_ref, ext_ref, o_ref, at_ref,
                h_scr, carry_scr, *, seq, tiles_per_seq):
    i, j = pl.program_id(0), pl.program_id(1)
    tm = x_ref.shape[0]

    @pl.when(j == 0)
    def _():
        x = x_ref[...]
        h_scr[...] = _rmsnorm(x, g_ref[...]).astype(BF16)
        o_ref[...] = x

    h = h_scr[...]
    a = _dot_f32(h, wa_ref[...])
    gate = _dot_f32(h, wg_ref[...])
    row = _iota(a.shape, 0)
    r1 = pltpu.roll(a, 1, 0)
    r2 = pltpu.roll(a, 2, 0)
    if tiles_per_seq:
        @pl.when((i == 0) & (j == 0))
        def _():
            carry_scr[...] = jnp.zeros_like(carry_scr)

        prev = carry_scr[j]
        seq_start = i % tiles_per_seq == 0
        p1 = jnp.where(seq_start, 0.0, prev[7:8])
        p2 = jnp.where(seq_start, 0.0, prev[6:7])
        a1 = jnp.where(row == 0, p1, r1)
        a2 = jnp.where(row == 0, p2, jnp.where(row == 1, p1, r2))
        carry_scr[j] = a[tm - 8:tm]
    else:
        ext = ext_ref[...]
        pos = row % seq
        a1 = jnp.where(pos == 0, pltpu.roll(ext, tm - 1, 0), r1)
        a2 = jnp.where(pos < 2, ext, r2)
    cw = cw_ref[...]
    conv = cb_ref[...] + cw[0:1] * a2 + cw[1:2] * a1 + cw[2:3] * a
    act = conv * _sigmoid(conv) * gate
    o_ref[...] += _dot_f32(act.astype(BF16), wd_ref[...])
    at_ref[...] = a[tm - at_ref.shape[0]:]


def _ffn(x, g, w_up, conv_w, conv_b, w_down, ext, seq):
    n, d = x.shape
    d_ff = w_down.shape[0]
    tf = 256
    assert d_ff % tf == 0
    nj = d_ff // tf
    if ext is None:
        tm = _pick_tile(seq, 1040)
        tiles_per_seq = seq // tm
        out_rows = 8
        ext = jnp.zeros((8, d_ff), F32)
        ext_spec = pl.BlockSpec((8, tf), lambda i, j: (0, j))
    else:
        tm = _pick_tile(n, 1040)
        assert tm % seq == 0 and seq == 8
        tiles_per_seq = 0
        out_rows = tm
        ext_spec = pl.BlockSpec((tm, tf), lambda i, j: (i, j))
    nt = n // tm
    out, a_tail = pl.pallas_call(
        functools.partial(_ffn_kernel, seq=seq, tiles_per_seq=tiles_per_seq),
        grid=(nt, nj),
        in_specs=[
            pl.BlockSpec((tm, d), lambda i, j: (i, 0)),
            pl.BlockSpec((1, d), lambda i, j: (0, 0)),
            pl.BlockSpec((d, tf), lambda i, j: (0, j)),
            pl.BlockSpec((d, tf), lambda i, j: (0, j + nj)),
            pl.BlockSpec((CONV_W, tf), lambda i, j: (0, j)),
            pl.BlockSpec((1, tf), lambda i, j: (0, j)),
            pl.BlockSpec((tf, d), lambda i, j: (j, 0)),
            ext_spec,
        ],
        out_specs=[pl.BlockSpec((tm, d), lambda i, j: (i, 0)),
                   pl.BlockSpec((out_rows, tf), lambda i, j: (i, j))],
        out_shape=[jax.ShapeDtypeStruct((n, d), F32),
                   jax.ShapeDtypeStruct((nt * out_rows, d_ff), F32)],
        scratch_shapes=[pltpu.VMEM((tm, d), BF16), pltpu.VMEM((nj, 8, tf), F32)],
        compiler_params=_cparams("arbitrary", "arbitrary"),
        name="conv_ffn",
    )(x, g.reshape(1, d), w_up, w_up, conv_w, conv_b.reshape(1, d_ff), w_down, ext)
    return out, a_tail, tiles_per_seq


def _norm_kernel(x_ref, g_ref, o_ref):
    o_ref[...] = _rmsnorm(x_ref[...], g_ref[...])


def _final_norm(x, g):
    n, d = x.shape
    tm = _pick_tile(n, 1040)
    return pl.pallas_call(
        _norm_kernel,
        grid=(n // tm,),
        in_specs=[pl.BlockSpec((tm, d), lambda i: (i, 0)), pl.BlockSpec((1, d), lambda i: (0, 0))],
        out_specs=pl.BlockSpec((tm, d), lambda i: (i, 0)),
        out_shape=jax.ShapeDtypeStruct((n, d), F32),
        compiler_params=_cparams("parallel"),
        name="final_norm",
    )(x, g.reshape(1, d))


def _reorder_in_proj(w_in, b_in):
    n_hf = 4 * HG_W + 3 * FOX_W
    pad = LANES - FOX_HEADS

    def reorder(a):
        parts = [a[..., n_hf + FOX_HEADS:], a[..., :n_hf], a[..., n_hf:n_hf + FOX_HEADS],
                 jnp.zeros(a.shape[:-1] + (pad,), a.dtype)]
        return jnp.concatenate(parts, axis=-1)

    return reorder(w_in).astype(BF16), reorder(b_in)


def kernel(x_prompt, x_sample, cache_k, cache_v, cache_logf, state_hgrn, state_conv, page_table, meta_tokens, norm1, norm2, norm_f, w_in, b_in, hg_lower_bounds, hg_norm, w_oa, w_ob, w_o, w_up, conv_w, conv_b, w_down):
    nb, s_len, d = x_prompt.shape
    seq_p = s_len + N_META
    db, seq_s, _ = x_sample.shape
    depth = w_in.shape[0]
    d_ff = w_down.shape[1]
    n_pool, page = cache_k.shape[1], cache_k.shape[2]
    assert w_in.shape[2] == 4 * HG_W + 3 * FOX_W + FOX_HEADS + 2 * d and 2 * d == C_ZQ

    w_in_r, b_in_r = _reorder_in_proj(w_in, b_in)
    w_oa, w_ob, w_o, w_up, w_down = (w.astype(BF16) for w in (w_oa, w_ob, w_o, w_up, w_down))
    cache_kt = jnp.transpose(cache_k, (0, 1, 3, 4, 2))
    cache_vt = jnp.transpose(cache_v, (0, 1, 3, 4, 2))
    cache_lft = jnp.swapaxes(cache_logf, 2, 3)
    lbraw = hg_lower_bounds.astype(F32)

    meta = jnp.broadcast_to(meta_tokens.astype(x_prompt.dtype)[None], (nb, N_META, d))
    xp = jnp.concatenate([meta, x_prompt], axis=1).reshape(nb * seq_p, d)
    xs = x_sample.reshape(db * seq_s, d)

    outs = [[] for _ in range(10)]
    for l in range(depth):
        z = _inproj(xp, norm1[l], w_in_r[l], b_in_r[l])
        z3 = z.reshape(nb, seq_p, NZ)
        y_hg, s_new = _hgrn_prompt(z3, lbraw, hg_norm[l], l)
        lf, c = _fox_gate(z3)
        o_fox = _fox_attn(z3, c)
        xp = _mix(xp, y_hg.reshape(nb * seq_p, HG_W), o_fox.reshape(nb * seq_p, FOX_W), z, w_oa[l], w_ob[l], w_o[l])
        xp, a_tail, tps = _ffn(xp, norm2[l], w_up[l], conv_w[l], conv_b[l], w_down[l], None, seq_p)
        outs[0].append(z3[:, :, C_FK:C_FK + FOX_W].reshape(nb, seq_p, FOX_HEADS, FOX_DH))
        outs[1].append(z3[:, :, C_FV:C_FV + FOX_W].reshape(nb, seq_p, FOX_HEADS, FOX_DH))
        outs[2].append(lf[:, :, :FOX_HEADS])
        outs[3].append(s_new)
        outs[4].append(a_tail.reshape(nb, tps, 8, d_ff)[:, tps - 1, 8 - (CONV_W - 1):])
        z = _inproj(xs, norm1[l], w_in_r[l], b_in_r[l])
        y_hg, s_new = _hgrn_sample(z, state_hgrn[l], lbraw, hg_norm[l], l, seq_s)
        o_fox, lf = _fox_decode(z, cache_kt, cache_vt, cache_lft, page_table, l, seq_s)
        o_fox = jnp.swapaxes(o_fox, 1, 2).reshape(db * seq_s, FOX_W)
        xs = _mix(xs, y_hg, o_fox, z, w_oa[l], w_ob[l], w_o[l])
        ext = jnp.pad(state_conv[l], ((0, 0), (0, seq_s - (CONV_W - 1)), (0, 0))).reshape(db * seq_s, d_ff)
        xs, a_all, _ = _ffn(xs, norm2[l], w_up[l], conv_w[l], conv_b[l], w_down[l], ext, seq_s)
        z3 = z.reshape(db, seq_s, NZ)
        outs[5].append(z3[:, :, C_FK:C_FK + FOX_W].reshape(db, seq_s, FOX_HEADS, FOX_DH))
        outs[6].append(z3[:, :, C_FV:C_FV + FOX_W].reshape(db, seq_s, FOX_HEADS, FOX_DH))
        outs[7].append(lf.reshape(db, seq_s, LANES)[:, :, :FOX_HEADS])
        outs[8].append(s_new)
        outs[9].append(a_all.reshape(db, seq_s, d_ff)[:, seq_s - (CONV_W - 1):])
    y_prompt = _final_norm(xp, norm_f).reshape(nb, seq_p, d)[:, N_META:]
    y_sample = _final_norm(xs, norm_f).reshape(db, seq_s, d)
    return (y_prompt, y_sample) + tuple(jnp.stack(o) for o in outs)
```

```python
import functools

import jax
import jax.numpy as jnp
from jax import lax
from jax.experimental import pallas as pl
from jax.experimental.pallas import tpu as pltpu

F32 = jnp.float32
BF16 = jnp.bfloat16
EPS = 1e-6
NEG_INF = float("-inf")
LOG2E = 1.4426950408889634

N_META = 16
HG_HEADS = 4
HG_DK = 128
FOX_HEADS = 8
FOX_DH = 64
CONV_W = 3

LANES = 128
HG_EXACT = 8
VMEM_LIMIT = 56 * 1024 * 1024

HG_W = HG_HEADS * HG_DK
FOX_W = FOX_HEADS * FOX_DH
C_GA = 0
C_GB = 1024
C_ZQ = 2048
C_ZF = C_ZQ + HG_W
C_ZI = C_ZF + HG_W
C_ZG = C_ZI + HG_W
C_FQ = C_ZG + HG_W
C_FK = C_FQ + FOX_W
C_FV = C_FK + FOX_W
C_FF = C_FV + FOX_W
NZ = C_FF + LANES


def _cparams(*sem):
    return pltpu.CompilerParams(dimension_semantics=sem, vmem_limit_bytes=VMEM_LIMIT)


def _pick_tile(n, cap):
    best = None
    for d in range(8, min(n, cap) + 1, 8):
        if n % d == 0:
            best = d
    assert best is not None, (n, cap)
    return best


def _sigmoid(x):
    return 1.0 / (1.0 + jnp.exp(-x))


def _log_sigmoid(x):
    return jnp.minimum(x, 0.0) - jnp.log1p(jnp.exp(-jnp.abs(x)))


def _split3(x):
    hi = x.astype(BF16)
    r = x - hi.astype(F32)
    mid = r.astype(BF16)
    r = r - mid.astype(F32)
    return hi, mid, r.astype(BF16)


def _dot_f32(a, b, dims=None):
    if dims is None:
        return jnp.dot(a, b, preferred_element_type=F32)
    return lax.dot_general(a, b, (dims, ((), ())), preferred_element_type=F32)


NT = ((1,), (1,))


def _mask_dot_left(m, x):
    return sum(_dot_f32(m, p) for p in _split3(x))


def _mask_dot_right(x, m):
    return sum(_dot_f32(p, m) for p in _split3(x))


def _rmsnorm(x, g):
    return x * lax.rsqrt(jnp.mean(x * x, axis=-1, keepdims=True) + EPS) * g


def _iota(shape, dim):
    return lax.broadcasted_iota(jnp.int32, shape, dim)


def _inproj_kernel(x_ref, g_ref, w_ref, b_ref, z_ref, h_scr):
    @pl.when(pl.program_id(1) == 0)
    def _():
        h_scr[...] = _rmsnorm(x_ref[...], g_ref[...]).astype(BF16)

    z_ref[...] = _dot_f32(h_scr[...], w_ref[...]) + b_ref[...]


def _inproj(x, g, w, b):
    n, d = x.shape
    tm = _pick_tile(n, 1040)
    tn = 1152
    assert NZ % tn == 0
    return pl.pallas_call(
        _inproj_kernel,
        grid=(n // tm, NZ // tn),
        in_specs=[
            pl.BlockSpec((tm, d), lambda i, j: (i, 0)),
            pl.BlockSpec((1, d), lambda i, j: (0, 0)),
            pl.BlockSpec((d, tn), lambda i, j: (0, j)),
            pl.BlockSpec((1, tn), lambda i, j: (0, j)),
        ],
        out_specs=pl.BlockSpec((tm, tn), lambda i, j: (i, j)),
        out_shape=jax.ShapeDtypeStruct((n, NZ), F32),
        scratch_shapes=[pltpu.VMEM((tm, d), BF16)],
        compiler_params=_cparams("parallel", "arbitrary"),
        name="inproj",
    )(x, g.reshape(1, d), w, b.reshape(1, NZ))


def _hgrn_lower_bound(lbraw, layer):
    e = jnp.exp(lbraw - jnp.max(lbraw, axis=0, keepdims=True))
    sm = e / jnp.sum(e, axis=0, keepdims=True)
    lb = jnp.zeros_like(sm[0:1])
    for i in range(1, layer + 1):
        lb = lb + sm[i:i + 1]
    return lb


def _hgrn_gates(zq, zf, zi, lb):
    q = zq * _sigmoid(zq)
    a = jnp.log(lb)
    b = jnp.log1p(-lb) + _log_sigmoid(zf)
    logf = jnp.maximum(a, b) + jnp.log1p(jnp.exp(-jnp.abs(a - b)))
    k = (1.0 - lb) / (1.0 + jnp.exp(zf))
    return q, k, zi, logf


def _hgrn_diag(q, k, b, v):
    n = q.shape[0]
    rowid = _iota(q.shape, 0)
    acc = jnp.zeros_like(v)
    for s in range(n):
        d = jnp.where(rowid >= s, b - b[s:s + 1], NEG_INF)
        w = q * k[s:s + 1] * jnp.exp(d)
        acc = acc + jnp.sum(w, axis=-1, keepdims=True) * v[s:s + 1]
    return acc


def _hgrn_head_out(o, zg, gain):
    o = o * lax.rsqrt(jnp.mean(o * o, axis=-1, keepdims=True) + EPS)
    return o * gain * (zg * _sigmoid(zg))


def _pad_rows(x, rows):
    if x.shape[0] == rows:
        return x
    return jnp.concatenate([x, jnp.zeros((rows - x.shape[0], x.shape[1]), x.dtype)], axis=0)


def _hgrn_prompt_kernel(zq_ref, zf_ref, zi_ref, zg_ref, lbraw_ref, gain_ref, y_ref, s_ref, st_scr,
                        *, layer, seq):
    ck = LANES
    n_full, tail = divmod(seq, ck)
    assert tail % HG_EXACT == 0
    lb = _hgrn_lower_bound(lbraw_ref[...], layer)
    gain = gain_ref[...]
    row = _iota((ck, ck), 0)
    col = _iota((ck, ck), 1)
    ltri = jnp.where(col <= row, 1.0, 0.0).astype(BF16)
    levels = []
    half = HG_EXACT
    while half < ck:
        levels.append((half, row % (2 * half) >= half, row // (2 * half) == col // (2 * half)))
        half *= 2
    st_scr[...] = jnp.zeros_like(st_scr)

    def chunk(r0, n_rows):
        sl = pl.ds(r0, n_rows)
        q, k, v, logf = _hgrn_gates(zq_ref[0, sl, :], zf_ref[0, sl, :], zi_ref[0, sl, :], lb)
        q, k, v, logf = (_pad_rows(a, ck) for a in (q, k, v, logf))
        b = _mask_dot_left(ltri, logf)
        bend = b[ck - 1:ck]
        a = jnp.zeros((ck, ck), F32)
        for half, upper, same_blk in levels:
            blk = 2 * half
            bmid = jnp.concatenate(
                [jnp.broadcast_to(b[blk * i + half - 1:blk * i + half], (blk, ck)) for i in range(ck // blk)],
                axis=0)
            q2 = jnp.where(upper, q * jnp.exp(jnp.minimum(b - bmid, 0.0)), 0.0).astype(BF16)
            k2 = jnp.where(upper, 0.0, k * jnp.exp(jnp.minimum(bmid - b, 0.0))).astype(BF16)
            a = a + jnp.where(same_blk, _dot_f32(q2, k2, NT), 0.0)
        st = st_scr[...]
        o = _dot_f32(a.astype(BF16), v.astype(BF16))
        o = o + _dot_f32((q * jnp.exp(b)).astype(BF16), st.astype(BF16), NT)
        diag = [_hgrn_diag(*(x[HG_EXACT * i:HG_EXACT * (i + 1)] for x in (q, k, b, v)))
                for i in range(n_rows // HG_EXACT)]
        o = o[:n_rows] + jnp.concatenate(diag, axis=0)
        st_scr[...] = st * jnp.exp(bend) + _dot_f32(v.T.astype(BF16), (k * jnp.exp(bend - b)).astype(BF16))
        y_ref[0, sl, :] = _hgrn_head_out(o, zg_ref[0, sl, :], gain).astype(y_ref.dtype)

    def body(c, carry):
        chunk(pl.multiple_of(c * ck, ck), ck)
        return carry

    lax.fori_loop(0, n_full, body, 0)
    if tail:
        chunk(n_full * ck, tail)
    s_ref[0, 0] = st_scr[...].T


def _hgrn_prompt(z3, lbraw, gain, layer):
    nb, seq, _ = z3.shape

    def zspec(c0):
        return pl.BlockSpec((1, seq, LANES), lambda b, h, c0=c0: (b, 0, c0 // LANES + h))

    return pl.pallas_call(
        functools.partial(_hgrn_prompt_kernel, layer=layer, seq=seq),
        grid=(nb, HG_HEADS),
        in_specs=[zspec(C_ZQ), zspec(C_ZF), zspec(C_ZI), zspec(C_ZG),
                  pl.BlockSpec((lbraw.shape[0], LANES), lambda b, h: (0, h)),
                  pl.BlockSpec((1, LANES), lambda b, h: (0, h))],
        out_specs=[pl.BlockSpec((1, seq, LANES), lambda b, h: (b, 0, h)),
                   pl.BlockSpec((1, 1, HG_DK, HG_DK), lambda b, h: (b, h, 0, 0))],
        out_shape=[jax.ShapeDtypeStruct((nb, seq, HG_W), BF16),
                   jax.ShapeDtypeStruct((nb, HG_HEADS, HG_DK, HG_DK), F32)],
        scratch_shapes=[pltpu.VMEM((HG_DK, HG_DK), F32)],
        compiler_params=_cparams("parallel", "parallel"),
        name="hgrn_prompt",
    )(z3, z3, z3, z3, lbraw, gain.reshape(1, HG_W))


def _hgrn_sample_kernel(zq_ref, zf_ref, zi_ref, zg_ref, lbraw_ref, gain_ref, s0_ref, y_ref, s_ref,
                        *, layer, seq):
    rows = zq_ref.shape[0]
    n_seq = rows // seq
    lb = _hgrn_lower_bound(lbraw_ref[...], layer)
    q, k, v, logf = _hgrn_gates(zq_ref[...], zf_ref[...], zi_ref[...], lb)
    row = _iota((rows, rows), 0)
    col = _iota((rows, rows), 1)
    ltri = jnp.where((row // seq == col // seq) & (col <= row), 1.0, 0.0).astype(BF16)
    b = _mask_dot_left(ltri, logf)
    bend = jnp.concatenate(
        [jnp.broadcast_to(b[seq * s + seq - 1:seq * (s + 1)], (seq, LANES)) for s in range(n_seq)], axis=0)
    qt = (q * jnp.exp(b)).astype(BF16)
    kt = k * jnp.exp(bend - b)
    outs = []
    for s in range(n_seq):
        rs = slice(seq * s, seq * (s + 1))
        s0 = s0_ref[s, 0]
        outs.append(_dot_f32(qt[rs], s0.astype(BF16)) + _hgrn_diag(q[rs], k[rs], b[rs], v[rs]))
        x = jnp.concatenate([kt[rs], jnp.exp(bend[rs]), jnp.zeros((HG_DK - 2 * seq, LANES), F32)], axis=0)
        xt = x.T
        inc = _dot_f32(xt.astype(BF16), _pad_rows(v[rs], HG_DK).astype(BF16))
        s_ref[s, 0] = xt[:, seq:seq + 1] * s0 + inc
    o = jnp.concatenate(outs, axis=0)
    y_ref[...] = _hgrn_head_out(o, zg_ref[...], gain_ref[...]).astype(y_ref.dtype)


def _hgrn_sample(z, state, lbraw, gain, layer, seq):
    n = z.shape[0]
    rows = LANES
    assert n % rows == 0 and rows % seq == 0 and 2 * seq <= HG_DK
    sblk = rows // seq

    def zspec(c0):
        return pl.BlockSpec((rows, LANES), lambda i, h, c0=c0: (i, c0 // LANES + h))

    st_spec = pl.BlockSpec((sblk, 1, HG_DK, HG_DK), lambda i, h: (i, h, 0, 0))
    return pl.pallas_call(
        functools.partial(_hgrn_sample_kernel, layer=layer, seq=seq),
        grid=(n // rows, HG_HEADS),
        in_specs=[zspec(C_ZQ), zspec(C_ZF), zspec(C_ZI), zspec(C_ZG),
                  pl.BlockSpec((lbraw.shape[0], LANES), lambda i, h: (0, h)),
                  pl.BlockSpec((1, LANES), lambda i, h: (0, h)),
                  st_spec],
        out_specs=[pl.BlockSpec((rows, LANES), lambda i, h: (i, h)), st_spec],
        out_shape=[jax.ShapeDtypeStruct((n, HG_W), BF16),
                   jax.ShapeDtypeStruct(state.shape, F32)],
        compiler_params=_cparams("parallel", "parallel"),
        name="hgrn_sample",
    )(z, z, z, z, lbraw, gain.reshape(1, HG_W), state)


def _fox_gate_kernel(ff_ref, lf_ref, c_ref, *, seq):
    ck = LANES
    n_full, tail = divmod(seq, ck)
    ltri = jnp.where(_iota((ck, ck), 1) <= _iota((ck, ck), 0), 1.0, 0.0).astype(BF16)

    def chunk(r0, n_rows, carry):
        sl = pl.ds(r0, n_rows)
        lf = _log_sigmoid(ff_ref[0, sl, :])
        lf_ref[0, sl, :] = lf
        c = _mask_dot_left(ltri, _pad_rows(lf, ck)) + carry
        c_ref[0, sl, :] = c[:n_rows]
        return c[n_rows - 1:n_rows]

    carry = lax.fori_loop(0, n_full, lambda i, cr: chunk(pl.multiple_of(i * ck, ck), ck, cr),
                          jnp.zeros((1, LANES), F32))
    if tail:
        chunk(n_full * ck, tail, carry)


def _fox_gate(z3):
    nb, seq, _ = z3.shape
    spec = pl.BlockSpec((1, seq, LANES), lambda b: (b, 0, 0))
    return pl.pallas_call(
        functools.partial(_fox_gate_kernel, seq=seq),
        grid=(nb,),
        in_specs=[pl.BlockSpec((1, seq, LANES), lambda b: (b, 0, C_FF // LANES))],
        out_specs=[spec, spec],
        out_shape=[jax.ShapeDtypeStruct((nb, seq, LANES), F32)] * 2,
        compiler_params=_cparams("parallel"),
        name="fox_gate",
    )(z3)


def _fox_attn_kernel(fq_ref, fk_ref, fv_ref, c_ref, o_ref, qa_scr, ka_scr, va_scr, r_scr, acc_scr,
                     *, seq, bq):
    pair = pl.program_id(1)
    padded = qa_scr.shape[1]
    nq = padded // bq
    ck = LANES
    n_full, tail = divmod(seq, ck)
    scale = FOX_DH ** -0.5 * LOG2E

    def build(r0, n_rows):
        sl = pl.ds(r0, n_rows)
        qf, kf, vf, cb = fq_ref[0, sl, :], fk_ref[0, sl, :], fv_ref[0, sl, :], c_ref[0, sl, :]
        lane = _iota((n_rows, LANES), 1)
        for hh in range(2):
            ch = jnp.sum(jnp.where(lane == 2 * pair + hh, cb, 0.0), axis=-1, keepdims=True) * LOG2E
            hi, mid, lo = (p.astype(F32) for p in _split3(ch))
            own = (lane >= FOX_DH * hh) & (lane < FOX_DH * (hh + 1))
            f0 = FOX_DH * (1 - hh)
            ones_q = (lane >= f0 + 3) & (lane < f0 + 6)
            ones_k = (lane >= f0) & (lane < f0 + 3)
            qa = jnp.where(own, qf * scale,
                           jnp.where(lane == f0, hi, jnp.where(lane == f0 + 1, mid, jnp.where(
                               lane == f0 + 2, lo, jnp.where(ones_q, 1.0, 0.0)))))
            ka = jnp.where(own, kf,
                           jnp.where(ones_k, 1.0, jnp.where(lane == f0 + 3, -hi, jnp.where(
                               lane == f0 + 4, -mid, jnp.where(lane == f0 + 5, -lo, 0.0)))))
            qa_scr[hh, sl, :] = qa.astype(BF16)
            ka_scr[hh, sl, :] = ka.astype(BF16)
            va_scr[hh, sl, :] = jnp.where(own, vf, 1.0).astype(BF16)

    def build_body(i, carry):
        build(pl.multiple_of(i * ck, ck), ck)
        return carry

    lax.fori_loop(0, n_full, build_body, 0)
    if tail:
        build(n_full * ck, tail)
    if padded > seq:
        zpad = jnp.zeros((padded - seq, LANES), BF16)
        for hh in range(2):
            qa_scr[hh, seq:padded, :] = zpad
            ka_scr[hh, seq:padded, :] = zpad
            va_scr[hh, seq:padded, :] = zpad

    lane_q = _iota((bq, LANES), 1)
    kc = 2 * bq

    def masked(width, first_free):
        return _iota((bq, width), 1) <= _iota((bq, width), 0) + first_free

    def qblock(q0, n_chunks, tail_k0, tail_w, n_store):
        qa = [qa_scr[hh, pl.ds(q0, bq), :] for hh in range(2)]
        tail_mask = masked(tail_w, tail_w - bq)

        def scores(hh, k0, width, mask):
            s = _dot_f32(qa[hh], ka_scr[hh, pl.ds(k0, width), :], NT)
            return s if mask is None else jnp.where(mask, s, NEG_INF)

        def max_step(k0, width, mask):
            for hh in range(2):
                s = scores(hh, k0, width, mask)
                out = r_scr[hh]
                for t in range(width // LANES):
                    out = jnp.maximum(out, s[:, LANES * t:LANES * (t + 1)])
                r_scr[hh] = out

        def acc_step(k0, width, mask):
            for hh in range(2):
                mb = r_scr[hh]
                p = jnp.exp2(scores(hh, k0, width, mask) - jnp.concatenate([mb] * (width // LANES), axis=1))
                acc_scr[hh] += _dot_f32(p.astype(BF16), va_scr[hh, pl.ds(k0, width), :])

        def loop(step):
            def body(j, carry):
                step(pl.multiple_of(j * kc, kc), kc, None)
                return carry
            lax.fori_loop(0, n_chunks, body, 0)
            step(tail_k0, tail_w, tail_mask)

        r_scr[...] = jnp.full(r_scr.shape, NEG_INF, F32)
        acc_scr[...] = jnp.zeros_like(acc_scr)
        loop(max_step)
        for hh in range(2):
            r_scr[hh] = jnp.broadcast_to(jnp.max(r_scr[hh], axis=-1, keepdims=True), (bq, LANES))
        loop(acc_step)
        outs = [acc_scr[hh] / pltpu.roll(acc_scr[hh], FOX_DH, 1) for hh in range(2)]
        o = jnp.where(lane_q < FOX_DH, outs[0], outs[1])
        o_ref[0, pl.ds(q0, n_store), :] = o[:n_store].astype(o_ref.dtype)

    def even_block(u, n_store=bq):
        q0 = pl.multiple_of(u * kc, kc) if not isinstance(u, int) else u * kc
        qblock(q0, u, q0, bq, n_store)

    def odd_block(u, n_store=bq):
        k0 = pl.multiple_of(u * kc, kc) if not isinstance(u, int) else u * kc
        qblock(k0 + bq, u, k0, kc, n_store)

    def pair_body(u, carry):
        even_block(u)
        odd_block(u)
        return carry

    n_whole = seq // bq
    lax.fori_loop(0, n_whole // 2, pair_body, 0)
    rest = [(i, bq) for i in range(n_whole - n_whole % 2, n_whole)]
    if n_whole < nq:
        rest.append((n_whole, seq - n_whole * bq))
    for i, n_store in rest:
        (odd_block if i % 2 else even_block)(i // 2, n_store)


def _fox_attn(z3, c):
    nb, seq, _ = z3.shape
    bq = 256
    padded = -(-seq // bq) * bq

    def zspec(c0):
        return pl.BlockSpec((1, seq, LANES), lambda b, p, c0=c0: (b, 0, c0 // LANES + p))

    return pl.pallas_call(
        functools.partial(_fox_attn_kernel, seq=seq, bq=bq),
        grid=(nb, FOX_HEADS // 2),
        in_specs=[zspec(C_FQ), zspec(C_FK), zspec(C_FV),
                  pl.BlockSpec((1, seq, LANES), lambda b, p: (b, 0, 0))],
        out_specs=pl.BlockSpec((1, seq, LANES), lambda b, p: (b, 0, p)),
        out_shape=jax.ShapeDtypeStruct((nb, seq, FOX_W), BF16),
        scratch_shapes=[pltpu.VMEM((2, padded, LANES), BF16), pltpu.VMEM((2, padded, LANES), BF16),
                        pltpu.VMEM((2, padded, LANES), BF16),
                        pltpu.VMEM((2, bq, LANES), F32), pltpu.VMEM((2, bq, LANES), F32)],
        compiler_params=_cparams("parallel", "parallel"),
        name="fox_attn",
    )(z3, z3, z3, c)


def _fox_decode_kernel(pt_ref, fq_ref, fk_ref, fv_ref, ff_ref, *rest, n_pages, page, seq):
    del pt_ref
    k_refs, v_refs, lf_refs = rest[:n_pages], rest[n_pages:2 * n_pages], rest[2 * n_pages:3 * n_pages]
    o_ref, lfo_ref = rest[3 * n_pages:]
    assert page == LANES
    rows = FOX_HEADS * seq
    lfn = _log_sigmoid(ff_ref[...])
    lfo_ref[...] = lfn
    lfn = jnp.where(_iota(lfn.shape, 1) < FOX_HEADS, lfn, 0.0)
    lfn_t = _pad_rows(lfn, LANES).T[:FOX_HEADS]
    stack = jnp.concatenate([r[0, 0] for r in lf_refs] + [lfn_t], axis=0)
    ustrict = jnp.where(_iota((LANES, LANES), 0) > _iota((LANES, LANES), 1), 1.0, 0.0).astype(BF16)
    within = _mask_dot_right(stack, ustrict)
    tot = jnp.sum(stack, axis=-1, keepdims=True)
    g = [None] * (n_pages + 1)
    run = jnp.zeros((FOX_HEADS, 1), F32)
    for p in reversed(range(n_pages + 1)):
        rs = slice(FOX_HEADS * p, FOX_HEADS * (p + 1))
        g[p] = within[rs] + run
        run = run + tot[rs]
    gq = _pad_rows(g[n_pages], LANES).T[:seq]
    lane = _iota((seq, LANES), 1)
    gq_col = jnp.concatenate(
        [jnp.sum(jnp.where(lane == h, gq, 0.0), axis=-1, keepdims=True) for h in range(FOX_HEADS)], axis=0)

    def bias(gp):
        return jnp.concatenate(
            [jnp.broadcast_to(gp[h:h + 1], (seq, page)) for h in range(FOX_HEADS)], axis=0) - gq_col

    own = _iota((rows, FOX_W), 0) // seq == _iota((rows, FOX_W), 1) // FOX_DH
    q = fq_ref[...] * (FOX_DH ** -0.5)
    q_bd = jnp.where(own, jnp.concatenate([q] * FOX_HEADS, axis=0), 0.0).astype(BF16)
    s_blocks = []
    for p in range(n_pages):
        kt = k_refs[p][0, 0].reshape(FOX_W, page).astype(BF16)
        s_blocks.append(_dot_f32(q_bd, kt) + bias(g[p]))
    s = _dot_f32(q_bd, _pad_rows(fk_ref[...], page).astype(BF16), NT) + bias(g[n_pages])
    s_blocks.append(jnp.where(_iota((rows, page), 1) <= _iota((rows, page), 0) % seq, s, NEG_INF))
    mx = s_blocks[0]
    for s in s_blocks[1:]:
        mx = jnp.maximum(mx, s)
    m = jnp.max(mx, axis=-1, keepdims=True)
    wsum = jnp.zeros((rows, page), F32)
    acc = jnp.zeros((rows, FOX_W), F32)
    for p in range(n_pages + 1):
        pe = jnp.exp(s_blocks[p] - m)
        wsum = wsum + pe
        if p < n_pages:
            acc = acc + _dot_f32(pe.astype(BF16), v_refs[p][0, 0].reshape(FOX_W, page).astype(BF16), NT)
        else:
            acc = acc + _dot_f32(pe.astype(BF16), _pad_rows(fv_ref[...], page).astype(BF16))
    acc = jnp.where(own, acc / jnp.sum(wsum, axis=-1, keepdims=True), 0.0)
    o = acc[:seq]
    for h in range(1, FOX_HEADS):
        o = o + acc[seq * h:seq * (h + 1)]
    o_ref[...] = o


def _fox_decode(z, cache_kt, cache_vt, cache_lft, page_table, layer, seq):
    n = z.shape[0]
    nb, n_pages = page_table.shape
    page = cache_lft.shape[-1]
    assert nb * seq == n and seq == 8

    def zspec(c0, w):
        return pl.BlockSpec((seq, w), lambda b, pt, c0=c0, w=w: (b, c0 // w))

    def pspec(shape, p):
        return pl.BlockSpec(shape, lambda b, pt, p=p: (layer, pt[b, p]) + (0,) * (len(shape) - 2))

    kv_shape = (1, 1, FOX_HEADS, FOX_DH, page)
    lf_shape = (1, 1, FOX_HEADS, page)
    in_specs = [zspec(C_FQ, FOX_W), zspec(C_FK, FOX_W), zspec(C_FV, FOX_W), zspec(C_FF, LANES)]
    in_specs += [pspec(kv_shape, p) for p in range(n_pages)]
    in_specs += [pspec(kv_shape, p) for p in range(n_pages)]
    in_specs += [pspec(lf_shape, p) for p in range(n_pages)]
    grid_spec = pltpu.PrefetchScalarGridSpec(
        num_scalar_prefetch=1,
        grid=(nb,),
        in_specs=in_specs,
        out_specs=[pl.BlockSpec((seq, FOX_W), lambda b, pt: (b, 0)),
                   pl.BlockSpec((seq, LANES), lambda b, pt: (b, 0))],
    )
    return pl.pallas_call(
        functools.partial(_fox_decode_kernel, n_pages=n_pages, page=page, seq=seq),
        grid_spec=grid_spec,
        out_shape=[jax.ShapeDtypeStruct((n, FOX_W), F32),
                   jax.ShapeDtypeStruct((n, LANES), F32)],
        compiler_params=_cparams("arbitrary"),
        name="fox_decode",
    )(page_table, z, z, z, z, *([cache_kt] * n_pages), *([cache_vt] * n_pages), *([cache_lft] * n_pages))


def _mix_kernel(x_ref, ya_ref, yb_ref, ga_ref, gb_ref, woa_ref, wob_ref, wo_ref, o_ref):
    ya = _dot_f32(ya_ref[...].astype(BF16), woa_ref[...])
    yb = _dot_f32(yb_ref[...].astype(BF16), wob_ref[...])
    merged = _sigmoid(ga_ref[...]) * ya + _sigmoid(gb_ref[...]) * yb
    o_ref[...] = x_ref[...] + _dot_f32(merged.astype(BF16), wo_ref[...])


def _mix(x, ya, yb, z, woa, wob, wo):
    n, d = x.shape
    assert d == C_GB - C_GA
    tm = _pick_tile(n, 704)
    row = lambda w: pl.BlockSpec((tm, w), lambda i: (i, 0))
    full = lambda a: pl.BlockSpec(a.shape, lambda i: (0, 0))
    return pl.pallas_call(
        _mix_kernel,
        grid=(n // tm,),
        in_specs=[row(d), row(HG_W), row(FOX_W),
                  pl.BlockSpec((tm, d), lambda i: (i, C_GA // d)),
                  pl.BlockSpec((tm, d), lambda i: (i, C_GB // d)),
                  full(woa), full(wob), full(wo)],
        out_specs=row(d),
        out_shape=jax.ShapeDtypeStruct((n, d), F32),
        compiler_params=_cparams("parallel"),
        name="mix_out",
    )(x, ya, yb, z, z, woa, wob, wo)


def _ffn_kernel(x_ref, g_ref, wa_ref, wg_ref, cw_ref, cb_ref, wd_ref, ext_ref, o_ref, at_ref,
                h_scr, carry_scr, *, seq, tiles_per_seq):
    i, j = pl.program_id(0), pl.program_id(1)
    tm = x_ref.shape[0]

    @pl.when(j == 0)
    def _():
        x = x_ref[...]
        h_scr[...] = _rmsnorm(x, g_ref[...]).astype(BF16)
        o_ref[...] = x

    h = h_scr[...]
    a = _dot_f32(h, wa_ref[...])
    gate = _dot_f32(h, wg_ref[...])
    row = _iota(a.shape, 0)
    r1 = pltpu.roll(a, 1, 0)
    r2 = pltpu.roll(a, 2, 0)
    if tiles_per_seq:
        @pl.when((i == 0) & (j == 0))
        def _():
            carry_scr[...] = jnp.zeros_like(carry_scr)

        prev = carry_scr[j]
        seq_start = i % tiles_per_seq == 0
        p1 = jnp.where(seq_start, 0.0, prev[7:8])
        p2 = jnp.where(seq_start, 0.0, prev[6:7])
        a1 = jnp.where(row == 0, p1, r1)
        a2 = jnp.where(row == 0, p2, jnp.where(row == 1, p1, r2))
        carry_scr[j] = a[tm - 8:tm]
    else:
        ext = ext_ref[...]
        pos = row % seq
        a1 = jnp.where(pos == 0, pltpu.roll(ext, tm - 1, 0), r1)
        a2 = jnp.where(pos < 2, ext, r2)
    cw = cw_ref[...]
    conv = cb_ref[...] + cw[0:1] * a2 + cw[1:2] * a1 + cw[2:3] * a
    act = conv * _sigmoid(conv) * gate
    o_ref[...] += _dot_f32(act.astype(BF16), wd_ref[...])
    at_ref[...] = a[tm - at_ref.shape[0]:]


def _ffn(x, g, w_up, conv_w, conv_b, w_down, ext, seq):
    n, d = x.shape
    d_ff = w_down.shape[0]
    tf = 256
    assert d_ff % tf == 0
    nj = d_ff // tf
    if ext is None:
        tm = _pick_tile(seq, 1040)
        tiles_per_seq = seq // tm
        out_rows = 8
        ext = jnp.zeros((8, d_ff), F32)
        ext_spec = pl.BlockSpec((8, tf), lambda i, j: (0, j))
    else:
        tm = _pick_tile(n, 1040)
        assert tm % seq == 0 and seq == 8
        tiles_per_seq = 0
        out_rows = tm
        ext_spec = pl.BlockSpec((tm, tf), lambda i, j: (i, j))
    nt = n // tm
    out, a_tail = pl.pallas_call(
        functools.partial(_ffn_kernel, seq=seq, tiles_per_seq=tiles_per_seq),
        grid=(nt, nj),
        in_specs=[
            pl.BlockSpec((tm, d), lambda i, j: (i, 0)),
            pl.BlockSpec((1, d), lambda i, j: (0, 0)),
            pl.BlockSpec((d, tf), lambda i, j: (0, j)),
            pl.BlockSpec((d, tf), lambda i, j: (0, j + nj)),
            pl.BlockSpec((CONV_W, tf), lambda i, j: (0, j)),
            pl.BlockSpec((1, tf), lambda i, j: (0, j)),
            pl.BlockSpec((tf, d), lambda i, j: (j, 0)),
            ext_spec,
        ],
        out_specs=[pl.BlockSpec((tm, d), lambda i, j: (i, 0)),
                   pl.BlockSpec((out_rows, tf), lambda i, j: (i, j))],
        out_shape=[jax.ShapeDtypeStruct((n, d), F32),
                   jax.ShapeDtypeStruct((nt * out_rows, d_ff), F32)],
        scratch_shapes=[pltpu.VMEM((tm, d), BF16), pltpu.VMEM((nj, 8, tf), F32)],
        compiler_params=_cparams("arbitrary", "arbitrary"),
        name="conv_ffn",
    )(x, g.reshape(1, d), w_up, w_up, conv_w, conv_b.reshape(1, d_ff), w_down, ext)
    return out, a_tail, tiles_per_seq


def _norm_kernel(x_ref, g_ref, o_ref):
    o_ref[...] = _rmsnorm(x_ref[...], g_ref[...])


def _final_norm(x, g):
    n, d = x.shape
    tm = _pick_tile(n, 1040)
    return pl.pallas_call(
        _norm_kernel,
        grid=(n // tm,),
        in_specs=[pl.BlockSpec((tm, d), lambda i: (i, 0)), pl.BlockSpec((1, d), lambda i: (0, 0))],
        out_specs=pl.BlockSpec((tm, d), lambda i: (i, 0)),
        out_shape=jax.ShapeDtypeStruct((n, d), F32),
        compiler_params=_cparams("parallel"),
        name="final_norm",
    )(x, g.reshape(1, d))


def _reorder_in_proj(w_in, b_in):
    n_hf = 4 * HG_W + 3 * FOX_W
    pad = LANES - FOX_HEADS

    def reorder(a):
        parts = [a[..., n_hf + FOX_HEADS:], a[..., :n_hf], a[..., n_hf:n_hf + FOX_HEADS],
                 jnp.zeros(a.shape[:-1] + (pad,), a.dtype)]
        return jnp.concatenate(parts, axis=-1)

    return reorder(w_in).astype(BF16), reorder(b_in)


def kernel(x_prompt, x_sample, cache_k, cache_v, cache_logf, state_hgrn, state_conv, page_table, meta_tokens, norm1, norm2, norm_f, w_in, b_in, hg_lower_bounds, hg_norm, w_oa, w_ob, w_o, w_up, conv_w, conv_b, w_down):
    nb, s_len, d = x_prompt.shape
    seq_p = s_len + N_META
    db, seq_s, _ = x_sample.shape
    depth = w_in.shape[0]
    d_ff = w_down.shape[1]
    n_pool, page = cache_k.shape[1], cache_k.shape[2]
    assert w_in.shape[2] == 4 * HG_W + 3 * FOX_W + FOX_HEADS + 2 * d and 2 * d == C_ZQ

    w_in_r, b_in_r = _reorder_in_proj(w_in, b_in)
    w_oa, w_ob, w_o, w_up, w_down = (w.astype(BF16) for w in (w_oa, w_ob, w_o, w_up, w_down))
    cache_kt = jnp.transpose(cache_k, (0, 1, 3, 4, 2))
    cache_vt = jnp.transpose(cache_v, (0, 1, 3, 4, 2))
    cache_lft = jnp.swapaxes(cache_logf, 2, 3)
    lbraw = hg_lower_bounds.astype(F32)

    meta = jnp.broadcast_to(meta_tokens.astype(x_prompt.dtype)[None], (nb, N_META, d))
    xp = jnp.concatenate([meta, x_prompt], axis=1).reshape(nb * seq_p, d)
    xs = x_sample.reshape(db * seq_s, d)

    outs = [[] for _ in range(10)]
    for l in range(depth):
        z = _inproj(xp, norm1[l], w_in_r[l], b_in_r[l])
        z3 = z.reshape(nb, seq_p, NZ)
        y_hg, s_new = _hgrn_prompt(z3, lbraw, hg_norm[l], l)
        lf, c = _fox_gate(z3)
        o_fox = _fox_attn(z3, c)
        xp = _mix(xp, y_hg.reshape(nb * seq_p, HG_W), o_fox.reshape(nb * seq_p, FOX_W), z, w_oa[l], w_ob[l], w_o[l])
        xp, a_tail, tps = _ffn(xp, norm2[l], w_up[l], conv_w[l], conv_b[l], w_down[l], None, seq_p)
        outs[0].append(z3[:, :, C_FK:C_FK + FOX_W].reshape(nb, seq_p, FOX_HEADS, FOX_DH))
        outs[1].append(z3[:, :, C_FV:C_FV + FOX_W].reshape(nb, seq_p, FOX_HEADS, FOX_DH))
        outs[2].append(lf[:, :, :FOX_HEADS])
        outs[3].append(s_new)
        outs[4].append(a_tail.reshape(nb, tps, 8, d_ff)[:, tps - 1, 8 - (CONV_W - 1):])
        z = _inproj(xs, norm1[l], w_in_r[l], b_in_r[l])
        y_hg, s_new = _hgrn_sample(z, state_hgrn[l], lbraw, hg_norm[l], l, seq_s)
        o_fox, lf = _fox_decode(z, cache_kt, cache_vt, cache_lft, page_table, l, seq_s)
        xs = _mix(xs, y_hg, o_fox, z, w_oa[l], w_ob[l], w_o[l])
        ext = jnp.pad(state_conv[l], ((0, 0), (0, seq_s - (CONV_W - 1)), (0, 0))).reshape(db * seq_s, d_ff)
        xs, a_all, _ = _ffn(xs, norm2[l], w_up[l], conv_w[l], conv_b[l], w_down[l], ext, seq_s)
        z3 = z.reshape(db, seq_s, NZ)
        outs[5].append(z3[:, :, C_FK:C_FK + FOX_W].reshape(db, seq_s, FOX_HEADS, FOX_DH))
        outs[6].append(z3[:, :, C_FV:C_FV + FOX_W].reshape(db, seq_s, FOX_HEADS, FOX_DH))
        outs[7].append(lf.reshape(db, seq_s, LANES)[:, :, :FOX_HEADS])
        outs[8].append(s_new)
        outs[9].append(a_all.reshape(db, seq_s, d_ff)[:, seq_s - (CONV_W - 1):])
    y_prompt = _final_norm(xp, norm_f).reshape(nb, seq_p, d)[:, N_META:]
    y_sample = _final_norm(xs, norm_f).reshape(db, seq_s, d)
    return (y_prompt, y_sample) + tuple(jnp.stack(o) for o in outs)
```

```python
import functools

import jax
import jax.numpy as jnp
from jax import lax
from jax.experimental import pallas as pl
from jax.experimental.pallas import tpu as pltpu

F32 = jnp.float32
BF16 = jnp.bfloat16
EPS = 1e-6
NEG_INF = float("-inf")
LOG2E = 1.4426950408889634

N_META = 16
HG_HEADS = 4
HG_DK = 128
FOX_HEADS = 8
FOX_DH = 64
CONV_W = 3

LANES = 128
HG_EXACT = 8
VMEM_LIMIT = 56 * 1024 * 1024

HG_W = HG_HEADS * HG_DK
FOX_W = FOX_HEADS * FOX_DH
C_GA = 0
C_GB = 1024
C_ZQ = 2048
C_ZF = C_ZQ + HG_W
C_ZI = C_ZF + HG_W
C_ZG = C_ZI + HG_W
C_FQ = C_ZG + HG_W
C_FK = C_FQ + FOX_W
C_FV = C_FK + FOX_W
C_FF = C_FV + FOX_W
NZ = C_FF + LANES


def _cparams(*sem):
    return pltpu.CompilerParams(dimension_semantics=sem, vmem_limit_bytes=VMEM_LIMIT)


def _pick_tile(n, cap):
    best = None
    for d in range(8, min(n, cap) + 1, 8):
        if n % d == 0:
            best = d
    assert best is not None, (n, cap)
    return best


def _sigmoid(x):
    return 0.5 * jnp.tanh(0.5 * x) + 0.5


def _log_sigmoid(x):
    return jnp.minimum(x, 0.0) - jnp.log1p(jnp.exp(-jnp.abs(x)))


def _split3(x):
    hi = x.astype(BF16)
    r = x - hi.astype(F32)
    mid = r.astype(BF16)
    r = r - mid.astype(F32)
    return hi, mid, r.astype(BF16)


def _dot_f32(a, b, dims=None):
    if dims is None:
        return jnp.dot(a, b, preferred_element_type=F32)
    return lax.dot_general(a, b, (dims, ((), ())), preferred_element_type=F32)


NT = ((1,), (1,))


def _mask_dot_left(m, x):
    return sum(_dot_f32(m, p) for p in _split3(x))


def _mask_dot_right(x, m):
    return sum(_dot_f32(p, m) for p in _split3(x))


def _rmsnorm(x, g):
    return x * lax.rsqrt(jnp.mean(x * x, axis=-1, keepdims=True) + EPS) * g


def _iota(shape, dim):
    return lax.broadcasted_iota(jnp.int32, shape, dim)


def _inproj_kernel(x_ref, g_ref, w_ref, b_ref, k_hbm, v_hbm, z_ref, k_ref, v_ref, h_scr, *, tn):
    del k_hbm, v_hbm
    j = pl.program_id(1)

    @pl.when(j == 0)
    def _():
        h_scr[...] = _rmsnorm(x_ref[...], g_ref[...]).astype(BF16)

    z = _dot_f32(h_scr[...], w_ref[...]) + b_ref[...]
    z_ref[...] = z

    @pl.when(j == C_FK // tn)
    def _():
        k_ref[...] = z[:, C_FK % tn:C_FK % tn + FOX_W]
        v_ref[...] = z[:, C_FV % tn:C_FV % tn + FOX_W]


def _inproj(x, g, w, b, k_all, v_all, layer):
    n, d = x.shape
    tm = _pick_tile(n, 1040)
    tn = 1152
    assert NZ % tn == 0 and C_FK // tn == (C_FV + FOX_W - 1) // tn
    kv_spec = pl.BlockSpec((None, tm, FOX_W), lambda i, j: (layer, i, 0))
    return pl.pallas_call(
        functools.partial(_inproj_kernel, tn=tn),
        grid=(n // tm, NZ // tn),
        in_specs=[
            pl.BlockSpec((tm, d), lambda i, j: (i, 0)),
            pl.BlockSpec((1, d), lambda i, j: (0, 0)),
            pl.BlockSpec((d, tn), lambda i, j: (0, j)),
            pl.BlockSpec((1, tn), lambda i, j: (0, j)),
            pl.BlockSpec(memory_space=pl.ANY),
            pl.BlockSpec(memory_space=pl.ANY),
        ],
        out_specs=[pl.BlockSpec((tm, tn), lambda i, j: (i, j)), kv_spec, kv_spec],
        out_shape=[jax.ShapeDtypeStruct((n, NZ), F32),
                   jax.ShapeDtypeStruct(k_all.shape, F32), jax.ShapeDtypeStruct(v_all.shape, F32)],
        input_output_aliases={4: 1, 5: 2},
        scratch_shapes=[pltpu.VMEM((tm, d), BF16)],
        compiler_params=_cparams("parallel", "arbitrary"),
        name="inproj",
    )(x, g.reshape(1, d), w, b.reshape(1, NZ), k_all, v_all)


def _hgrn_lower_bound(lbraw, layer):
    e = jnp.exp(lbraw - jnp.max(lbraw, axis=0, keepdims=True))
    sm = e / jnp.sum(e, axis=0, keepdims=True)
    lb = jnp.zeros_like(sm[0:1])
    for i in range(1, layer + 1):
        lb = lb + sm[i:i + 1]
    return lb


def _hgrn_gates(zq, zf, zi, lb):
    q = zq * _sigmoid(zq)
    a = jnp.log(lb)
    b = jnp.log1p(-lb) + _log_sigmoid(zf)
    logf = jnp.maximum(a, b) + jnp.log1p(jnp.exp(-jnp.abs(a - b)))
    k = (1.0 - lb) * _sigmoid(-zf)
    return q, k, zi, logf


def _hgrn_diag(q, k, b, v):
    n = q.shape[0]
    rowid = _iota(q.shape, 0)
    acc = jnp.zeros_like(v)
    for s in range(n):
        d = jnp.where(rowid >= s, b - b[s:s + 1], NEG_INF)
        w = q * k[s:s + 1] * jnp.exp(d)
        acc = acc + jnp.sum(w, axis=-1, keepdims=True) * v[s:s + 1]
    return acc


def _hgrn_head_out(o, zg, gain):
    o = o * lax.rsqrt(jnp.mean(o * o, axis=-1, keepdims=True) + EPS)
    return o * gain * (zg * _sigmoid(zg))


def _pad_rows(x, rows):
    if x.shape[0] == rows:
        return x
    return jnp.concatenate([x, jnp.zeros((rows - x.shape[0], x.shape[1]), x.dtype)], axis=0)


def _hgrn_prompt_kernel(zq_ref, zf_ref, zi_ref, zg_ref, lbraw_ref, gain_ref, y_ref, s_ref, st_scr,
                        *, layer, seq):
    ck = LANES
    n_full, tail = divmod(seq, ck)
    assert tail % HG_EXACT == 0
    lb = _hgrn_lower_bound(lbraw_ref[...], layer)
    gain = gain_ref[...]
    row = _iota((ck, ck), 0)
    col = _iota((ck, ck), 1)
    ltri = jnp.where(col <= row, 1.0, 0.0).astype(BF16)
    levels = []
    half = HG_EXACT
    while half < ck:
        levels.append((half, row % (2 * half) >= half, row // (2 * half) == col // (2 * half)))
        half *= 2
    st_scr[...] = jnp.zeros_like(st_scr)

    def chunk(r0, n_rows):
        sl = pl.ds(r0, n_rows)
        q, k, v, logf = _hgrn_gates(zq_ref[0, sl, :], zf_ref[0, sl, :], zi_ref[0, sl, :], lb)
        q, k, v, logf = (_pad_rows(a, ck) for a in (q, k, v, logf))
        b = _mask_dot_left(ltri, logf)
        bend = b[ck - 1:ck]
        a = jnp.zeros((ck, ck), F32)
        for half, upper, same_blk in levels:
            blk = 2 * half
            bmid = jnp.concatenate(
                [jnp.broadcast_to(b[blk * i + half - 1:blk * i + half], (blk, ck)) for i in range(ck // blk)],
                axis=0)
            q2 = jnp.where(upper, q * jnp.exp(jnp.minimum(b - bmid, 0.0)), 0.0).astype(BF16)
            k2 = jnp.where(upper, 0.0, k * jnp.exp(jnp.minimum(bmid - b, 0.0))).astype(BF16)
            a = a + jnp.where(same_blk, _dot_f32(q2, k2, NT), 0.0)
        st = st_scr[...]
        o = _dot_f32(a.astype(BF16), v.astype(BF16))
        o = o + _dot_f32((q * jnp.exp(b)).astype(BF16), st.astype(BF16), NT)
        diag = [_hgrn_diag(*(x[HG_EXACT * i:HG_EXACT * (i + 1)] for x in (q, k, b, v)))
                for i in range(n_rows // HG_EXACT)]
        o = o[:n_rows] + jnp.concatenate(diag, axis=0)
        st_scr[...] = st * jnp.exp(bend) + _dot_f32(v.T.astype(BF16), (k * jnp.exp(bend - b)).astype(BF16))
        y_ref[0, sl, :] = _hgrn_head_out(o, zg_ref[0, sl, :], gain).astype(y_ref.dtype)

    def body(c, carry):
        chunk(pl.multiple_of(c * ck, ck), ck)
        return carry

    lax.fori_loop(0, n_full, body, 0)
    if tail:
        chunk(n_full * ck, tail)
    s_ref[0, 0] = st_scr[...].T


def _hgrn_prompt(z3, lbraw, gain, layer):
    nb, seq, _ = z3.shape

    def zspec(c0):
        return pl.BlockSpec((1, seq, LANES), lambda b, h, c0=c0: (b, 0, c0 // LANES + h))

    return pl.pallas_call(
        functools.partial(_hgrn_prompt_kernel, layer=layer, seq=seq),
        grid=(nb, HG_HEADS),
        in_specs=[zspec(C_ZQ), zspec(C_ZF), zspec(C_ZI), zspec(C_ZG),
                  pl.BlockSpec((lbraw.shape[0], LANES), lambda b, h: (0, h)),
                  pl.BlockSpec((1, LANES), lambda b, h: (0, h))],
        out_specs=[pl.BlockSpec((1, seq, LANES), lambda b, h: (b, 0, h)),
                   pl.BlockSpec((1, 1, HG_DK, HG_DK), lambda b, h: (b, h, 0, 0))],
        out_shape=[jax.ShapeDtypeStruct((nb, seq, HG_W), BF16),
                   jax.ShapeDtypeStruct((nb, HG_HEADS, HG_DK, HG_DK), F32)],
        scratch_shapes=[pltpu.VMEM((HG_DK, HG_DK), F32)],
        compiler_params=_cparams("parallel", "parallel"),
        name="hgrn_prompt",
    )(z3, z3, z3, z3, lbraw, gain.reshape(1, HG_W))


def _hgrn_sample_kernel(zq_ref, zf_ref, zi_ref, zg_ref, lbraw_ref, gain_ref, s0_ref, y_ref, s_ref,
                        *, layer, seq):
    rows = zq_ref.shape[0]
    n_seq = rows // seq
    lb = _hgrn_lower_bound(lbraw_ref[...], layer)
    q, k, v, logf = _hgrn_gates(zq_ref[...], zf_ref[...], zi_ref[...], lb)
    row = _iota((rows, rows), 0)
    col = _iota((rows, rows), 1)
    ltri = jnp.where((row // seq == col // seq) & (col <= row), 1.0, 0.0).astype(BF16)
    b = _mask_dot_left(ltri, logf)
    bend = jnp.concatenate(
        [jnp.broadcast_to(b[seq * s + seq - 1:seq * (s + 1)], (seq, LANES)) for s in range(n_seq)], axis=0)
    qt = (q * jnp.exp(b)).astype(BF16)
    kt = k * jnp.exp(bend - b)
    outs = []
    for s in range(n_seq):
        rs = slice(seq * s, seq * (s + 1))
        s0 = s0_ref[s, 0]
        outs.append(_dot_f32(qt[rs], s0.astype(BF16)) + _hgrn_diag(q[rs], k[rs], b[rs], v[rs]))
        x = jnp.concatenate([kt[rs], jnp.exp(bend[rs]), jnp.zeros((HG_DK - 2 * seq, LANES), F32)], axis=0)
        xt = x.T
        inc = _dot_f32(xt.astype(BF16), _pad_rows(v[rs], HG_DK).astype(BF16))
        s_ref[s, 0] = xt[:, seq:seq + 1] * s0 + inc
    o = jnp.concatenate(outs, axis=0)
    y_ref[...] = _hgrn_head_out(o, zg_ref[...], gain_ref[...]).astype(y_ref.dtype)


def _hgrn_sample(z, state, lbraw, gain, layer, seq):
    n = z.shape[0]
    rows = LANES
    assert n % rows == 0 and rows % seq == 0 and 2 * seq <= HG_DK
    sblk = rows // seq

    def zspec(c0):
        return pl.BlockSpec((rows, LANES), lambda i, h, c0=c0: (i, c0 // LANES + h))

    st_spec = pl.BlockSpec((sblk, 1, HG_DK, HG_DK), lambda i, h: (i, h, 0, 0))
    return pl.pallas_call(
        functools.partial(_hgrn_sample_kernel, layer=layer, seq=seq),
        grid=(n // rows, HG_HEADS),
        in_specs=[zspec(C_ZQ), zspec(C_ZF), zspec(C_ZI), zspec(C_ZG),
                  pl.BlockSpec((lbraw.shape[0], LANES), lambda i, h: (0, h)),
                  pl.BlockSpec((1, LANES), lambda i, h: (0, h)),
                  pl.BlockSpec((None, sblk, 1, HG_DK, HG_DK), lambda i, h: (layer, i, h, 0, 0))],
        out_specs=[pl.BlockSpec((rows, LANES), lambda i, h: (i, h)), st_spec],
        out_shape=[jax.ShapeDtypeStruct((n, HG_W), BF16),
                   jax.ShapeDtypeStruct(state.shape[1:], F32)],
        compiler_params=_cparams("parallel", "parallel"),
        name="hgrn_sample",
    )(z, z, z, z, lbraw, gain.reshape(1, HG_W), state)


def _fox_gate_kernel(ff_ref, lf_ref, c_ref, *, seq):
    ck = LANES
    n_full, tail = divmod(seq, ck)
    ltri = jnp.where(_iota((ck, ck), 1) <= _iota((ck, ck), 0), 1.0, 0.0).astype(BF16)

    def chunk(r0, n_rows, carry):
        sl = pl.ds(r0, n_rows)
        lf = _log_sigmoid(ff_ref[0, sl, :])
        lf_ref[0, sl, :] = lf
        c = _mask_dot_left(ltri, _pad_rows(lf, ck)) + carry
        c_ref[0, sl, :] = c[:n_rows]
        return c[n_rows - 1:n_rows]

    carry = lax.fori_loop(0, n_full, lambda i, cr: chunk(pl.multiple_of(i * ck, ck), ck, cr),
                          jnp.zeros((1, LANES), F32))
    if tail:
        chunk(n_full * ck, tail, carry)


def _fox_gate(z3):
    nb, seq, _ = z3.shape
    spec = pl.BlockSpec((1, seq, LANES), lambda b: (b, 0, 0))
    return pl.pallas_call(
        functools.partial(_fox_gate_kernel, seq=seq),
        grid=(nb,),
        in_specs=[pl.BlockSpec((1, seq, LANES), lambda b: (b, 0, C_FF // LANES))],
        out_specs=[spec, spec],
        out_shape=[jax.ShapeDtypeStruct((nb, seq, LANES), F32)] * 2,
        compiler_params=_cparams("parallel"),
        name="fox_gate",
    )(z3)


def _fox_attn_kernel(fq_ref, fk_ref, fv_ref, c_ref, o_ref, qa_scr, ka_scr, va_scr, r_scr, acc_scr,
                     *, seq, bq):
    padded = qa_scr.shape[1]
    nq = padded // bq
    ck = LANES
    n_full, tail = divmod(seq, ck)
    scale = FOX_DH ** -0.5 * LOG2E

    def build(r0, n_rows):
        sl = pl.ds(r0, n_rows)
        cb = c_ref[0, sl, :]
        lane = _iota((n_rows, LANES), 1)
        for h in range(FOX_HEADS):
            hh = h % 2
            ps = slice(LANES * (h // 2), LANES * (h // 2 + 1))
            qf, kf, vf = fq_ref[0, sl, ps], fk_ref[0, sl, ps], fv_ref[0, sl, ps]
            ch = jnp.sum(jnp.where(lane == h, cb, 0.0), axis=-1, keepdims=True) * LOG2E
            hi, mid, lo = (p.astype(F32) for p in _split3(ch))
            own = (lane >= FOX_DH * hh) & (lane < FOX_DH * (hh + 1))
            f0 = FOX_DH * (1 - hh)
            ones_q = (lane >= f0 + 3) & (lane < f0 + 6)
            ones_k = (lane >= f0) & (lane < f0 + 3)
            qa = jnp.where(own, qf * scale,
                           jnp.where(lane == f0, hi, jnp.where(lane == f0 + 1, mid, jnp.where(
                               lane == f0 + 2, lo, jnp.where(ones_q, 1.0, 0.0)))))
            ka = jnp.where(own, kf,
                           jnp.where(ones_k, 1.0, jnp.where(lane == f0 + 3, -hi, jnp.where(
                               lane == f0 + 4, -mid, jnp.where(lane == f0 + 5, -lo, 0.0)))))
            qa_scr[h, sl, :] = qa.astype(BF16)
            ka_scr[h, sl, :] = ka.astype(BF16)
            va_scr[h, sl, :] = jnp.where(own, vf, 1.0).astype(BF16)

    def build_body(i, carry):
        build(pl.multiple_of(i * ck, ck), ck)
        return carry

    lax.fori_loop(0, n_full, build_body, 0)
    if tail:
        build(n_full * ck, tail)
    if padded > seq:
        zpad = jnp.zeros((padded - seq, LANES), BF16)
        for h in range(FOX_HEADS):
            qa_scr[h, seq:padded, :] = zpad
            ka_scr[h, seq:padded, :] = zpad
            va_scr[h, seq:padded, :] = zpad

    lane_q = _iota((bq, LANES), 1)
    kc = 2 * bq

    def masked(width, first_free):
        return _iota((bq, width), 1) <= _iota((bq, width), 0) + first_free

    def qblock(q0, n_chunks, tail_k0, tail_w, n_store):
        qa = [qa_scr[h, pl.ds(q0, bq), :] for h in range(FOX_HEADS)]
        tail_mask = masked(tail_w, tail_w - bq)

        def scores(hh, k0, width, mask):
            s = _dot_f32(qa[hh], ka_scr[hh, pl.ds(k0, width), :], NT)
            return s if mask is None else jnp.where(mask, s, NEG_INF)

        def step(k0, width, mask):
            for hh in range(FOX_HEADS):
                s = scores(hh, k0, width, mask)
                m_old = r_scr[hh]
                blk = s[:, :LANES]
                for t in range(1, width // LANES):
                    blk = jnp.maximum(blk, s[:, LANES * t:LANES * (t + 1)])
                m_new = jnp.maximum(m_old, jnp.broadcast_to(jnp.max(blk, axis=-1, keepdims=True), (bq, LANES)))
                p = jnp.exp2(s - jnp.concatenate([m_new] * (width // LANES), axis=1))
                acc_scr[hh] = jnp.exp2(m_old - m_new) * acc_scr[hh] + _dot_f32(
                    p.astype(BF16), va_scr[hh, pl.ds(k0, width), :])
                r_scr[hh] = m_new

        def body(j, carry):
            step(pl.multiple_of(j * kc, kc), kc, None)
            return carry

        r_scr[...] = jnp.full(r_scr.shape, NEG_INF, F32)
        acc_scr[...] = jnp.zeros_like(acc_scr)
        lax.fori_loop(0, n_chunks, body, 0)
        step(tail_k0, tail_w, tail_mask)
        outs = [acc_scr[h] / pltpu.roll(acc_scr[h], FOX_DH, 1) for h in range(FOX_HEADS)]
        for pr in range(FOX_HEADS // 2):
            o = jnp.where(lane_q < FOX_DH, outs[2 * pr], outs[2 * pr + 1])
            o_ref[0, pl.ds(q0, n_store), LANES * pr:LANES * (pr + 1)] = o[:n_store].astype(o_ref.dtype)

    def even_block(u, n_store=bq):
        q0 = pl.multiple_of(u * kc, kc) if not isinstance(u, int) else u * kc
        qblock(q0, u, q0, bq, n_store)

    def odd_block(u, n_store=bq):
        k0 = pl.multiple_of(u * kc, kc) if not isinstance(u, int) else u * kc
        qblock(k0 + bq, u, k0, kc, n_store)

    def pair_body(u, carry):
        even_block(u)
        odd_block(u)
        return carry

    n_whole = seq // bq
    lax.fori_loop(0, n_whole // 2, pair_body, 0)
    rest = [(i, bq) for i in range(n_whole - n_whole % 2, n_whole)]
    if n_whole < nq:
        rest.append((n_whole, seq - n_whole * bq))
    for i, n_store in rest:
        (odd_block if i % 2 else even_block)(i // 2, n_store)


def _fox_attn(z3, c):
    nb, seq, _ = z3.shape
    bq = 256
    padded = -(-seq // bq) * bq

    def zspec(c0):
        return pl.BlockSpec((1, seq, FOX_W), lambda b, c0=c0: (b, 0, c0 // FOX_W))

    return pl.pallas_call(
        functools.partial(_fox_attn_kernel, seq=seq, bq=bq),
        grid=(nb,),
        in_specs=[zspec(C_FQ), zspec(C_FK), zspec(C_FV),
                  pl.BlockSpec((1, seq, LANES), lambda b: (b, 0, 0))],
        out_specs=pl.BlockSpec((1, seq, FOX_W), lambda b: (b, 0, 0)),
        out_shape=jax.ShapeDtypeStruct((nb, seq, FOX_W), BF16),
        scratch_shapes=[pltpu.VMEM((FOX_HEADS, padded, LANES), BF16)] * 3
        + [pltpu.VMEM((FOX_HEADS, bq, LANES), F32)] * 2,
        compiler_params=_cparams("parallel"),
        name="fox_attn",
    )(z3, z3, z3, c)


def _fox_decode_kernel(pt_ref, fq_ref, fk_ref, fv_ref, ff_ref, *rest, n_pages, page, seq):
    del pt_ref
    k_refs, v_refs, lf_refs = rest[:n_pages], rest[n_pages:2 * n_pages], rest[2 * n_pages:3 * n_pages]
    o_ref, lfo_ref = rest[3 * n_pages:]
    assert page == LANES
    rows = FOX_HEADS * seq
    lfn = _log_sigmoid(ff_ref[...])
    lfo_ref[...] = lfn
    lfn = jnp.where(_iota(lfn.shape, 1) < FOX_HEADS, lfn, 0.0)
    lfn_t = _pad_rows(lfn, LANES).T[:FOX_HEADS]
    stack = jnp.concatenate([r[0, 0] for r in lf_refs] + [lfn_t], axis=0)
    ustrict = jnp.where(_iota((LANES, LANES), 0) > _iota((LANES, LANES), 1), 1.0, 0.0).astype(BF16)
    within = _mask_dot_right(stack, ustrict)
    tot = jnp.sum(stack, axis=-1, keepdims=True)
    g = [None] * (n_pages + 1)
    run = jnp.zeros((FOX_HEADS, 1), F32)
    for p in reversed(range(n_pages + 1)):
        rs = slice(FOX_HEADS * p, FOX_HEADS * (p + 1))
        g[p] = within[rs] + run
        run = run + tot[rs]
    gq = _pad_rows(g[n_pages], LANES).T[:seq]
    lane = _iota((seq, LANES), 1)
    gq_col = jnp.concatenate(
        [jnp.sum(jnp.where(lane == h, gq, 0.0), axis=-1, keepdims=True) for h in range(FOX_HEADS)], axis=0)

    def bias(gp):
        return jnp.concatenate(
            [jnp.broadcast_to(gp[h:h + 1], (seq, page)) for h in range(FOX_HEADS)], axis=0) - gq_col

    own = _iota((rows, FOX_W), 0) // seq == _iota((rows, FOX_W), 1) // FOX_DH
    q = fq_ref[...] * (FOX_DH ** -0.5)
    q_bd = jnp.where(own, jnp.concatenate([q] * FOX_HEADS, axis=0), 0.0).astype(BF16)
    s_blocks = []
    for p in range(n_pages):
        kt = k_refs[p][0, 0].reshape(FOX_W, page).astype(BF16)
        s_blocks.append(_dot_f32(q_bd, kt) + bias(g[p]))
    s = _dot_f32(q_bd, _pad_rows(fk_ref[...], page).astype(BF16), NT) + bias(g[n_pages])
    s_blocks.append(jnp.where(_iota((rows, page), 1) <= _iota((rows, page), 0) % seq, s, NEG_INF))
    mx = s_blocks[0]
    for s in s_blocks[1:]:
        mx = jnp.maximum(mx, s)
    m = jnp.max(mx, axis=-1, keepdims=True)
    wsum = jnp.zeros((rows, page), F32)
    acc = jnp.zeros((rows, FOX_W), F32)
    for p in range(n_pages + 1):
        pe = jnp.exp(s_blocks[p] - m)
        wsum = wsum + pe
        if p < n_pages:
            acc = acc + _dot_f32(pe.astype(BF16), v_refs[p][0, 0].reshape(FOX_W, page).astype(BF16), NT)
        else:
            acc = acc + _dot_f32(pe.astype(BF16), _pad_rows(fv_ref[...], page).astype(BF16))
    acc = jnp.where(own, acc / jnp.sum(wsum, axis=-1, keepdims=True), 0.0)
    o = acc[:seq]
    for h in range(1, FOX_HEADS):
        o = o + acc[seq * h:seq * (h + 1)]
    o_ref[...] = o


def _fox_decode(z, cache_kt, cache_vt, cache_lft, page_table, layer, seq):
    n = z.shape[0]
    nb, n_pages = page_table.shape
    page = cache_lft.shape[-1]
    assert nb * seq == n and seq == 8

    def zspec(c0, w):
        return pl.BlockSpec((seq, w), lambda b, pt, c0=c0, w=w: (b, c0 // w))

    def pspec(shape, p):
        return pl.BlockSpec(shape, lambda b, pt, p=p: (layer, pt[b, p]) + (0,) * (len(shape) - 2))

    kv_shape = (1, 1, FOX_HEADS, FOX_DH, page)
    lf_shape = (1, 1, FOX_HEADS, page)
    in_specs = [zspec(C_FQ, FOX_W), zspec(C_FK, FOX_W), zspec(C_FV, FOX_W), zspec(C_FF, LANES)]
    in_specs += [pspec(kv_shape, p) for p in range(n_pages)]
    in_specs += [pspec(kv_shape, p) for p in range(n_pages)]
    in_specs += [pspec(lf_shape, p) for p in range(n_pages)]
    grid_spec = pltpu.PrefetchScalarGridSpec(
        num_scalar_prefetch=1,
        grid=(nb,),
        in_specs=in_specs,
        out_specs=[pl.BlockSpec((seq, FOX_W), lambda b, pt: (b, 0)),
                   pl.BlockSpec((seq, LANES), lambda b, pt: (b, 0))],
    )
    return pl.pallas_call(
        functools.partial(_fox_decode_kernel, n_pages=n_pages, page=page, seq=seq),
        grid_spec=grid_spec,
        out_shape=[jax.ShapeDtypeStruct((n, FOX_W), F32),
                   jax.ShapeDtypeStruct((n, LANES), F32)],
        compiler_params=_cparams("arbitrary"),
        name="fox_decode",
    )(page_table, z, z, z, z, *([cache_kt] * n_pages), *([cache_vt] * n_pages), *([cache_lft] * n_pages))


def _mix_kernel(x_ref, ya_ref, yb_ref, ga_ref, gb_ref, woa_ref, wob_ref, wo_ref, o_ref):
    ya = _dot_f32(ya_ref[...].astype(BF16), woa_ref[...])
    yb = _dot_f32(yb_ref[...].astype(BF16), wob_ref[...])
    merged = _sigmoid(ga_ref[...]) * ya + _sigmoid(gb_ref[...]) * yb
    o_ref[...] = x_ref[...] + _dot_f32(merged.astype(BF16), wo_ref[...])


def _mix(x, ya, yb, z, woa, wob, wo):
    n, d = x.shape
    assert d == C_GB - C_GA
    tm = _pick_tile(n, 704)
    row = lambda w: pl.BlockSpec((tm, w), lambda i: (i, 0))
    full = lambda a: pl.BlockSpec(a.shape, lambda i: (0, 0))
    return pl.pallas_call(
        _mix_kernel,
        grid=(n // tm,),
        in_specs=[row(d), row(HG_W), row(FOX_W),
                  pl.BlockSpec((tm, d), lambda i: (i, C_GA // d)),
                  pl.BlockSpec((tm, d), lambda i: (i, C_GB // d)),
                  full(woa), full(wob), full(wo)],
        out_specs=row(d),
        out_shape=jax.ShapeDtypeStruct((n, d), F32),
        compiler_params=_cparams("parallel"),
        name="mix_out",
    )(x, ya, yb, z, z, woa, wob, wo)


def _ffn_kernel(x_ref, g_ref, wa_ref, wg_ref, cw_ref, cb_ref, wd_ref, ext_ref, o_ref, at_ref,
                h_scr, carry_scr, *, seq, tiles_per_seq):
    i, j = pl.program_id(0), pl.program_id(1)
    tm = x_ref.shape[0]

    @pl.when(j == 0)
    def _():
        x = x_ref[...]
        h_scr[...] = _rmsnorm(x, g_ref[...]).astype(BF16)
        o_ref[...] = x

    h = h_scr[...]
    a = _dot_f32(h, wa_ref[...])
    gate = _dot_f32(h, wg_ref[...])
    row = _iota(a.shape, 0)
    r1 = pltpu.roll(a, 1, 0)
    r2 = pltpu.roll(a, 2, 0)
    if tiles_per_seq:
        @pl.when((i == 0) & (j == 0))
        def _():
            carry_scr[...] = jnp.zeros_like(carry_scr)

        prev = carry_scr[j]
        seq_start = i % tiles_per_seq == 0
        p1 = jnp.where(seq_start, 0.0, prev[7:8])
        p2 = jnp.where(seq_start, 0.0, prev[6:7])
        a1 = jnp.where(row == 0, p1, r1)
        a2 = jnp.where(row == 0, p2, jnp.where(row == 1, p1, r2))
        carry_scr[j] = a[tm - 8:tm]
    else:
        ext = ext_ref[...]
        pos = row % seq
        a1 = jnp.where(pos == 0, pltpu.roll(ext, tm - 1, 0), r1)
        a2 = jnp.where(pos < 2, ext, r2)
    cw = cw_ref[...]
    conv = cb_ref[...] + cw[0:1] * a2 + cw[1:2] * a1 + cw[2:3] * a
    act = conv * _sigmoid(conv) * gate
    o_ref[...] += _dot_f32(act.astype(BF16), wd_ref[...])
    at_ref[...] = a[tm - at_ref.shape[0]:]


def _ffn(x, g, w_up, conv_w, conv_b, w_down, ext, seq):
    n, d = x.shape
    d_ff = w_down.shape[0]
    tf = 256
    assert d_ff % tf == 0
    nj = d_ff // tf
    if ext is None:
        tm = _pick_tile(seq, 1040)
        tiles_per_seq = seq // tm
        out_rows = 8
        ext = jnp.zeros((8, d_ff), F32)
        ext_spec = pl.BlockSpec((8, tf), lambda i, j: (0, j))
    else:
        tm = _pick_tile(n, 1040)
        assert tm % seq == 0 and seq == 8
        tiles_per_seq = 0
        out_rows = tm
        ext_spec = pl.BlockSpec((tm, tf), lambda i, j: (i, j))
    nt = n // tm
    out, a_tail = pl.pallas_call(
        functools.partial(_ffn_kernel, seq=seq, tiles_per_seq=tiles_per_seq),
        grid=(nt, nj),
        in_specs=[
            pl.BlockSpec((tm, d), lambda i, j: (i, 0)),
            pl.BlockSpec((1, d), lambda i, j: (0, 0)),
            pl.BlockSpec((d, tf), lambda i, j: (0, j)),
            pl.BlockSpec((d, tf), lambda i, j: (0, j + nj)),
            pl.BlockSpec((CONV_W, tf), lambda i, j: (0, j)),
            pl.BlockSpec((1, tf), lambda i, j: (0, j)),
            pl.BlockSpec((tf, d), lambda i, j: (j, 0)),
            ext_spec,
        ],
        out_specs=[pl.BlockSpec((tm, d), lambda i, j: (i, 0)),
                   pl.BlockSpec((out_rows, tf), lambda i, j: (i, j))],
        out_shape=[jax.ShapeDtypeStruct((n, d), F32),
                   jax.ShapeDtypeStruct((nt * out_rows, d_ff), F32)],
        scratch_shapes=[pltpu.VMEM((tm, d), BF16), pltpu.VMEM((nj, 8, tf), F32)],
        compiler_params=_cparams("arbitrary", "arbitrary"),
        name="conv_ffn",
    )(x, g.reshape(1, d), w_up, w_up, conv_w, conv_b.reshape(1, d_ff), w_down, ext)
    return out, a_tail, tiles_per_seq


def _norm_kernel(x_ref, g_ref, o_ref):
    o_ref[...] = _rmsnorm(x_ref[...], g_ref[...])


def _final_norm(x, g):
    n, d = x.shape
    tm = _pick_tile(n, 1040)
    return pl.pallas_call(
        _norm_kernel,
        grid=(n // tm,),
        in_specs=[pl.BlockSpec((tm, d), lambda i: (i, 0)), pl.BlockSpec((1, d), lambda i: (0, 0))],
        out_specs=pl.BlockSpec((tm, d), lambda i: (i, 0)),
        out_shape=jax.ShapeDtypeStruct((n, d), F32),
        compiler_params=_cparams("parallel"),
        name="final_norm",
    )(x, g.reshape(1, d))


def _reorder_in_proj(w_in, b_in):
    n_hf = 4 * HG_W + 3 * FOX_W
    pad = LANES - FOX_HEADS

    def reorder(a):
        parts = [a[..., n_hf + FOX_HEADS:], a[..., :n_hf], a[..., n_hf:n_hf + FOX_HEADS],
                 jnp.zeros(a.shape[:-1] + (pad,), a.dtype)]
        return jnp.concatenate(parts, axis=-1)

    return reorder(w_in).astype(BF16), reorder(b_in)


def kernel(x_prompt, x_sample, cache_k, cache_v, cache_logf, state_hgrn, state_conv, page_table, meta_tokens, norm1, norm2, norm_f, w_in, b_in, hg_lower_bounds, hg_norm, w_oa, w_ob, w_o, w_up, conv_w, conv_b, w_down):
    nb, s_len, d = x_prompt.shape
    seq_p = s_len + N_META
    db, seq_s, _ = x_sample.shape
    depth = w_in.shape[0]
    d_ff = w_down.shape[1]
    n_pool, page = cache_k.shape[1], cache_k.shape[2]
    assert w_in.shape[2] == 4 * HG_W + 3 * FOX_W + FOX_HEADS + 2 * d and 2 * d == C_ZQ

    w_in_r, b_in_r = _reorder_in_proj(w_in, b_in)
    w_oa, w_ob, w_o, w_up, w_down = (w.astype(BF16) for w in (w_oa, w_ob, w_o, w_up, w_down))
    cache_kt = jnp.transpose(cache_k, (0, 1, 3, 4, 2))
    cache_vt = jnp.transpose(cache_v, (0, 1, 3, 4, 2))
    cache_lft = jnp.swapaxes(cache_logf, 2, 3)
    lbraw = hg_lower_bounds.astype(F32)

    meta = jnp.broadcast_to(meta_tokens.astype(x_prompt.dtype)[None], (nb, N_META, d))
    xp = jnp.concatenate([meta, x_prompt], axis=1).reshape(nb * seq_p, d)
    xs = x_sample.reshape(db * seq_s, d)

    outs = [[] for _ in range(6)]
    k_p, v_p = (jnp.zeros((depth, nb * seq_p, FOX_W), F32) for _ in range(2))
    k_s, v_s = (jnp.zeros((depth, db * seq_s, FOX_W), F32) for _ in range(2))
    for l in range(depth):
        z, k_p, v_p = _inproj(xp, norm1[l], w_in_r[l], b_in_r[l], k_p, v_p, l)
        z3 = z.reshape(nb, seq_p, NZ)
        y_hg, s_new = _hgrn_prompt(z3, lbraw, hg_norm[l], l)
        lf, c = _fox_gate(z3)
        o_fox = _fox_attn(z3, c)
        xp = _mix(xp, y_hg.reshape(nb * seq_p, HG_W), o_fox.reshape(nb * seq_p, FOX_W), z, w_oa[l], w_ob[l], w_o[l])
        xp, a_tail, tps = _ffn(xp, norm2[l], w_up[l], conv_w[l], conv_b[l], w_down[l], None, seq_p)
        outs[0].append(lf[:, :, :FOX_HEADS])
        outs[1].append(s_new)
        outs[2].append(a_tail.reshape(nb, tps, 8, d_ff)[:, tps - 1, 8 - (CONV_W - 1):])
        z, k_s, v_s = _inproj(xs, norm1[l], w_in_r[l], b_in_r[l], k_s, v_s, l)
        y_hg, s_new = _hgrn_sample(z, state_hgrn, lbraw, hg_norm[l], l, seq_s)
        o_fox, lf = _fox_decode(z, cache_kt, cache_vt, cache_lft, page_table, l, seq_s)
        xs = _mix(xs, y_hg, o_fox, z, w_oa[l], w_ob[l], w_o[l])
        ext = jnp.pad(state_conv[l], ((0, 0), (0, seq_s - (CONV_W - 1)), (0, 0))).reshape(db * seq_s, d_ff)
        xs, a_all, _ = _ffn(xs, norm2[l], w_up[l], conv_w[l], conv_b[l], w_down[l], ext, seq_s)
        outs[3].append(lf.reshape(db, seq_s, LANES)[:, :, :FOX_HEADS])
        outs[4].append(s_new)
        outs[5].append(a_all.reshape(db, seq_s, d_ff)[:, seq_s - (CONV_W - 1):])
    y_prompt = _final_norm(xp, norm_f).reshape(nb, seq_p, d)[:, N_META:]
    y_sample = _final_norm(xs, norm_f).reshape(db, seq_s, d)
    lf_p, hg_p, cv_p, lf_s, hg_s, cv_s = (jnp.stack(o) for o in outs)
    kv_p = (depth, nb, seq_p, FOX_HEADS, FOX_DH)
    kv_s = (depth, db, seq_s, FOX_HEADS, FOX_DH)
    return (y_prompt, y_sample, k_p.reshape(kv_p), v_p.reshape(kv_p), lf_p, hg_p, cv_p,
            k_s.reshape(kv_s), v_s.reshape(kv_s), lf_s, hg_s, cv_s)
```

```python
import functools

import jax
import jax.numpy as jnp
from jax import lax
from jax.experimental import pallas as pl
from jax.experimental.pallas import tpu as pltpu

F32 = jnp.float32
BF16 = jnp.bfloat16
EPS = 1e-6
NEG_INF = float("-inf")
LOG2E = 1.4426950408889634

N_META = 16
HG_HEADS = 4
HG_DK = 128
FOX_HEADS = 8
FOX_DH = 64
CONV_W = 3

LANES = 128
HG_EXACT = 8
HG_GROUP = 4
VMEM_LIMIT = 56 * 1024 * 1024

HG_W = HG_HEADS * HG_DK
FOX_W = FOX_HEADS * FOX_DH
C_GA = 0
C_GB = 1024
C_ZQ = 2048
C_ZF = C_ZQ + HG_W
C_ZI = C_ZF + HG_W
C_ZG = C_ZI + HG_W
C_FQ = C_ZG + HG_W
C_FK = C_FQ + FOX_W
C_FV = C_FK + FOX_W
C_FF = C_FV + FOX_W
NZ = C_FF + LANES


def _cparams(*sem):
    return pltpu.CompilerParams(dimension_semantics=sem, vmem_limit_bytes=VMEM_LIMIT)


def _pick_tile(n, cap):
    best = None
    for d in range(8, min(n, cap) + 1, 8):
        if n % d == 0:
            best = d
    assert best is not None, (n, cap)
    return best


def _sigmoid(x):
    return 0.5 * jnp.tanh(0.5 * x) + 0.5


def _log_sigmoid(x):
    return jnp.minimum(x, 0.0) - jnp.log1p(jnp.exp(-jnp.abs(x)))


def _split3(x):
    hi = x.astype(BF16)
    r = x - hi.astype(F32)
    mid = r.astype(BF16)
    r = r - mid.astype(F32)
    return hi, mid, r.astype(BF16)


def _dot_f32(a, b, dims=None):
    if dims is None:
        return jnp.dot(a, b, preferred_element_type=F32)
    return lax.dot_general(a, b, (dims, ((), ())), preferred_element_type=F32)


NT = ((1,), (1,))


def _mask_dot_left(m, x):
    return sum(_dot_f32(m, p) for p in _split3(x))


def _mask_dot_right(x, m):
    return sum(_dot_f32(p, m) for p in _split3(x))


def _rmsnorm(x, g):
    return x * lax.rsqrt(jnp.mean(x * x, axis=-1, keepdims=True) + EPS) * g


def _iota(shape, dim):
    return lax.broadcasted_iota(jnp.int32, shape, dim)


def _inproj_kernel(x_ref, g_ref, w_ref, b_ref, *rest, tn, with_kv):
    if with_kv:
        z_ref, k_ref, v_ref, h_scr = rest[2:]
    else:
        z_ref, h_scr = rest
    j = pl.program_id(1)

    @pl.when(j == 0)
    def _():
        h_scr[...] = _rmsnorm(x_ref[...], g_ref[...]).astype(BF16)

    z = _dot_f32(h_scr[...], w_ref[...]) + b_ref[...]
    z_ref[...] = z

    if with_kv:
        @pl.when(j == C_FK // tn)
        def _():
            k_ref[...] = z[:, C_FK % tn:C_FK % tn + FOX_W]
            v_ref[...] = z[:, C_FV % tn:C_FV % tn + FOX_W]


def _inproj(x, g, w, b, kv=None, layer=0):
    n, d = x.shape
    tm = _pick_tile(n, 1040)
    tn = 1152
    assert NZ % tn == 0 and C_FK // tn == (C_FV + FOX_W - 1) // tn
    in_specs = [
        pl.BlockSpec((tm, d), lambda i, j: (i, 0)),
        pl.BlockSpec((1, d), lambda i, j: (0, 0)),
        pl.BlockSpec((d, tn), lambda i, j: (0, j)),
        pl.BlockSpec((1, tn), lambda i, j: (0, j)),
    ]
    out_specs = [pl.BlockSpec((tm, tn), lambda i, j: (i, j))]
    out_shape = [jax.ShapeDtypeStruct((n, NZ), F32)]
    args = [x, g.reshape(1, d), w, b.reshape(1, NZ)]
    aliases = {}
    if kv is not None:
        in_specs += [pl.BlockSpec(memory_space=pl.ANY)] * 2
        out_specs += [pl.BlockSpec((None, tm, FOX_W), lambda i, j: (layer, i, 0))] * 2
        out_shape += [jax.ShapeDtypeStruct(a.shape, F32) for a in kv]
        args += list(kv)
        aliases = {4: 1, 5: 2}
    out = pl.pallas_call(
        functools.partial(_inproj_kernel, tn=tn, with_kv=kv is not None),
        grid=(n // tm, NZ // tn),
        in_specs=in_specs,
        out_specs=out_specs,
        out_shape=out_shape,
        input_output_aliases=aliases,
        scratch_shapes=[pltpu.VMEM((tm, d), BF16)],
        compiler_params=_cparams("parallel", "arbitrary"),
        name="inproj",
    )(*args)
    return out if kv is not None else out[0]


def _hgrn_lower_bound(lbraw, layer):
    e = jnp.exp(lbraw - jnp.max(lbraw, axis=0, keepdims=True))
    sm = e / jnp.sum(e, axis=0, keepdims=True)
    lb = jnp.zeros_like(sm[0:1])
    for i in range(1, layer + 1):
        lb = lb + sm[i:i + 1]
    return lb


def _hgrn_gates(zq, zf, zi, lb):
    q = zq * _sigmoid(zq)
    a = jnp.log(lb)
    b = jnp.log1p(-lb) + _log_sigmoid(zf)
    logf = jnp.maximum(a, b) + jnp.log1p(jnp.exp(-jnp.abs(a - b)))
    k = (1.0 - lb) * _sigmoid(-zf)
    return q, k, zi, logf


def _hgrn_diag(q, k, b, v):
    n = q.shape[0]
    rowid = _iota(q.shape, 0)
    acc = jnp.zeros_like(v)
    for s in range(n):
        d = jnp.where(rowid >= s, b - b[s:s + 1], NEG_INF)
        w = q * k[s:s + 1] * jnp.exp2(d)
        acc = acc + jnp.sum(w, axis=-1, keepdims=True) * v[s:s + 1]
    return acc


def _hgrn_head_out(o, zg, gain):
    o = o * lax.rsqrt(jnp.mean(o * o, axis=-1, keepdims=True) + EPS)
    return o * gain * (zg * _sigmoid(zg))


def _pad_rows(x, rows):
    if x.shape[0] == rows:
        return x
    return jnp.concatenate([x, jnp.zeros((rows - x.shape[0], x.shape[1]), x.dtype)], axis=0)


def _hgrn_prompt_kernel(zq_ref, zf_ref, zi_ref, zg_ref, lbraw_ref, gain_ref, y_ref, s_ref, st_scr,
                        *, layer, seq):
    ck = LANES
    n_full, tail = divmod(seq, ck)
    assert tail % HG_EXACT == 0
    lb_all = _hgrn_lower_bound(lbraw_ref[...], layer)
    gain_all = gain_ref[...]
    row = _iota((ck, ck), 0)
    col = _iota((ck, ck), 1)
    ltri = jnp.where(col <= row, 1.0, 0.0).astype(BF16)
    levels = []
    half = HG_EXACT
    while half < ck:
        levels.append((half, row % (2 * half) >= half, row // (2 * half) == col // (2 * half)))
        half *= 2
    st_scr[...] = jnp.zeros_like(st_scr)

    def chunk(r0, n_rows):
        for g in range(HG_GROUP):
            head_chunk(r0, n_rows, g, slice(HG_DK * g, HG_DK * (g + 1)))

    def head_chunk(r0, n_rows, g, hs):
        sl = pl.ds(r0, n_rows)
        lb, gain = lb_all[:, hs], gain_all[:, hs]
        q, k, v, logf = _hgrn_gates(zq_ref[0, sl, hs], zf_ref[0, sl, hs], zi_ref[0, sl, hs], lb)
        q, k, v, logf = (_pad_rows(a, ck) for a in (q, k, v, logf))
        b = _mask_dot_left(ltri, logf * LOG2E)
        bend = b[ck - 1:ck]
        a = jnp.zeros((ck, ck), F32)
        for half, upper, same_blk in levels:
            blk = 2 * half
            bmid = jnp.concatenate(
                [jnp.broadcast_to(b[blk * i + half - 1:blk * i + half], (blk, ck)) for i in range(ck // blk)],
                axis=0)
            q2 = jnp.where(upper, q * jnp.exp2(jnp.minimum(b - bmid, 0.0)), 0.0).astype(BF16)
            k2 = jnp.where(upper, 0.0, k * jnp.exp2(jnp.minimum(bmid - b, 0.0))).astype(BF16)
            a = a + jnp.where(same_blk, _dot_f32(q2, k2, NT), 0.0)
        st = st_scr[g]
        o = _dot_f32(a.astype(BF16), v.astype(BF16))
        o = o + _dot_f32((q * jnp.exp2(b)).astype(BF16), st.astype(BF16), NT)
        diag = [_hgrn_diag(*(x[HG_EXACT * i:HG_EXACT * (i + 1)] for x in (q, k, b, v)))
                for i in range(n_rows // HG_EXACT)]
        o = o[:n_rows] + jnp.concatenate(diag, axis=0)
        st_scr[g] = st * jnp.exp2(bend) + _dot_f32(v.T.astype(BF16), (k * jnp.exp2(bend - b)).astype(BF16))
        y_ref[0, sl, hs] = _hgrn_head_out(o, zg_ref[0, sl, hs], gain).astype(y_ref.dtype)

    def body(c, carry):
        chunk(pl.multiple_of(c * ck, ck), ck)
        return carry

    lax.fori_loop(0, n_full, body, 0)
    if tail:
        chunk(n_full * ck, tail)
    for g in range(HG_GROUP):
        s_ref[0, g] = st_scr[g].T


def _hgrn_prompt(z3, lbraw, gain, layer):
    nb, seq, _ = z3.shape
    gw = HG_GROUP * HG_DK

    def zspec(c0):
        return pl.BlockSpec((1, seq, gw), lambda b, h, c0=c0: (b, 0, c0 // gw + h))

    return pl.pallas_call(
        functools.partial(_hgrn_prompt_kernel, layer=layer, seq=seq),
        grid=(nb, HG_HEADS // HG_GROUP),
        in_specs=[zspec(C_ZQ), zspec(C_ZF), zspec(C_ZI), zspec(C_ZG),
                  pl.BlockSpec((lbraw.shape[0], gw), lambda b, h: (0, h)),
                  pl.BlockSpec((1, gw), lambda b, h: (0, h))],
        out_specs=[pl.BlockSpec((1, seq, gw), lambda b, h: (b, 0, h)),
                   pl.BlockSpec((1, HG_GROUP, HG_DK, HG_DK), lambda b, h: (b, h, 0, 0))],
        out_shape=[jax.ShapeDtypeStruct((nb, seq, HG_W), BF16),
                   jax.ShapeDtypeStruct((nb, HG_HEADS, HG_DK, HG_DK), F32)],
        scratch_shapes=[pltpu.VMEM((HG_GROUP, HG_DK, HG_DK), F32)],
        compiler_params=_cparams("parallel", "parallel"),
        name="hgrn_prompt",
    )(z3, z3, z3, z3, lbraw, gain.reshape(1, HG_W))


def _hgrn_sample_kernel(zq_ref, zf_ref, zi_ref, zg_ref, lbraw_ref, gain_ref, s0_ref, y_ref, s_ref,
                        *, layer, seq):
    rows = zq_ref.shape[0]
    n_seq = rows // seq
    lb = _hgrn_lower_bound(lbraw_ref[...], layer)
    q, k, v, logf = _hgrn_gates(zq_ref[...], zf_ref[...], zi_ref[...], lb)
    row = _iota((rows, rows), 0)
    col = _iota((rows, rows), 1)
    ltri = jnp.where((row // seq == col // seq) & (col <= row), 1.0, 0.0).astype(BF16)
    b = _mask_dot_left(ltri, logf * LOG2E)
    bend = jnp.concatenate(
        [jnp.broadcast_to(b[seq * s + seq - 1:seq * (s + 1)], (seq, LANES)) for s in range(n_seq)], axis=0)
    qt = (q * jnp.exp2(b)).astype(BF16)
    kt = k * jnp.exp2(bend - b)
    outs = []
    for s in range(n_seq):
        rs = slice(seq * s, seq * (s + 1))
        s0 = s0_ref[s, 0]
        outs.append(_dot_f32(qt[rs], s0.astype(BF16)) + _hgrn_diag(q[rs], k[rs], b[rs], v[rs]))
        x = jnp.concatenate([kt[rs], jnp.exp2(bend[rs]), jnp.zeros((HG_DK - 2 * seq, LANES), F32)], axis=0)
        xt = x.T
        inc = _dot_f32(xt.astype(BF16), _pad_rows(v[rs], HG_DK).astype(BF16))
        s_ref[s, 0] = xt[:, seq:seq + 1] * s0 + inc
    o = jnp.concatenate(outs, axis=0)
    y_ref[...] = _hgrn_head_out(o, zg_ref[...], gain_ref[...]).astype(y_ref.dtype)


def _hgrn_sample(z, state, lbraw, gain, layer, seq):
    n = z.shape[0]
    rows = LANES
    assert n % rows == 0 and rows % seq == 0 and 2 * seq <= HG_DK
    sblk = rows // seq

    def zspec(c0):
        return pl.BlockSpec((rows, LANES), lambda i, h, c0=c0: (i, c0 // LANES + h))

    st_spec = pl.BlockSpec((sblk, 1, HG_DK, HG_DK), lambda i, h: (i, h, 0, 0))
    return pl.pallas_call(
        functools.partial(_hgrn_sample_kernel, layer=layer, seq=seq),
        grid=(n // rows, HG_HEADS),
        in_specs=[zspec(C_ZQ), zspec(C_ZF), zspec(C_ZI), zspec(C_ZG),
                  pl.BlockSpec((lbraw.shape[0], LANES), lambda i, h: (0, h)),
                  pl.BlockSpec((1, LANES), lambda i, h: (0, h)),
                  pl.BlockSpec((None, sblk, 1, HG_DK, HG_DK), lambda i, h: (layer, i, h, 0, 0))],
        out_specs=[pl.BlockSpec((rows, LANES), lambda i, h: (i, h)), st_spec],
        out_shape=[jax.ShapeDtypeStruct((n, HG_W), BF16),
                   jax.ShapeDtypeStruct(state.shape[1:], F32)],
        compiler_params=_cparams("parallel", "parallel"),
        name="hgrn_sample",
    )(z, z, z, z, lbraw, gain.reshape(1, HG_W), state)


def _fox_gate_kernel(ff_ref, lf_ref, c_ref, *, seq):
    ck = LANES
    n_full, tail = divmod(seq, ck)
    ltri = jnp.where(_iota((ck, ck), 1) <= _iota((ck, ck), 0), 1.0, 0.0).astype(BF16)

    def chunk(r0, n_rows, carry):
        sl = pl.ds(r0, n_rows)
        lf = _log_sigmoid(ff_ref[0, sl, :])
        lf_ref[0, sl, :] = lf
        c = _mask_dot_left(ltri, _pad_rows(lf, ck)) + carry
        c_ref[0, sl, :] = c[:n_rows]
        return c[n_rows - 1:n_rows]

    carry = lax.fori_loop(0, n_full, lambda i, cr: chunk(pl.multiple_of(i * ck, ck), ck, cr),
                          jnp.zeros((1, LANES), F32))
    if tail:
        chunk(n_full * ck, tail, carry)


def _fox_gate(z3):
    nb, seq, _ = z3.shape
    spec = pl.BlockSpec((1, seq, LANES), lambda b: (b, 0, 0))
    return pl.pallas_call(
        functools.partial(_fox_gate_kernel, seq=seq),
        grid=(nb,),
        in_specs=[pl.BlockSpec((1, seq, LANES), lambda b: (b, 0, C_FF // LANES))],
        out_specs=[spec, spec],
        out_shape=[jax.ShapeDtypeStruct((nb, seq, LANES), F32)] * 2,
        compiler_params=_cparams("parallel"),
        name="fox_gate",
    )(z3)


def _fox_attn_kernel(fq_ref, fk_ref, fv_ref, c_ref, o_ref, qa_scr, ka_scr, va_scr, r_scr, acc_scr,
                     *, seq, bq):
    padded = qa_scr.shape[1]
    nq = padded // bq
    ck = LANES
    n_full, tail = divmod(seq, ck)
    scale = FOX_DH ** -0.5 * LOG2E

    def build(r0, n_rows):
        sl = pl.ds(r0, n_rows)
        cb = c_ref[0, sl, :]
        lane = _iota((n_rows, LANES), 1)
        for h in range(FOX_HEADS):
            hh = h % 2
            ps = slice(LANES * (h // 2), LANES * (h // 2 + 1))
            qf, kf, vf = fq_ref[0, sl, ps], fk_ref[0, sl, ps], fv_ref[0, sl, ps]
            ch = jnp.sum(jnp.where(lane == h, cb, 0.0), axis=-1, keepdims=True) * LOG2E
            hi, mid, lo = (p.astype(F32) for p in _split3(ch))
            own = (lane >= FOX_DH * hh) & (lane < FOX_DH * (hh + 1))
            f0 = FOX_DH * (1 - hh)
            ones_q = (lane >= f0 + 3) & (lane < f0 + 6)
            ones_k = (lane >= f0) & (lane < f0 + 3)
            qa = jnp.where(own, qf * scale,
                           jnp.where(lane == f0, hi, jnp.where(lane == f0 + 1, mid, jnp.where(
                               lane == f0 + 2, lo, jnp.where(ones_q, 1.0, 0.0)))))
            ka = jnp.where(own, kf,
                           jnp.where(ones_k, 1.0, jnp.where(lane == f0 + 3, -hi, jnp.where(
                               lane == f0 + 4, -mid, jnp.where(lane == f0 + 5, -lo, 0.0)))))
            qa_scr[h, sl, :] = qa.astype(BF16)
            ka_scr[h, sl, :] = ka.astype(BF16)
            va_scr[h, sl, :] = jnp.where(own, vf, 1.0).astype(BF16)

    def build_body(i, carry):
        build(pl.multiple_of(i * ck, ck), ck)
        return carry

    lax.fori_loop(0, n_full, build_body, 0)
    if tail:
        build(n_full * ck, tail)
    if padded > seq:
        zpad = jnp.zeros((padded - seq, LANES), BF16)
        for h in range(FOX_HEADS):
            qa_scr[h, seq:padded, :] = zpad
            ka_scr[h, seq:padded, :] = zpad
            va_scr[h, seq:padded, :] = zpad

    kc = 2 * bq

    def qblock(q0, n_chunks, tail_k0, tail_w, rows):
        qa = [qa_scr[h, pl.ds(q0, rows), :] for h in range(FOX_HEADS)]
        tail_mask = _iota((rows, tail_w), 1) <= _iota((rows, tail_w), 0) + (tail_w - bq)
        lane_q = _iota((rows, LANES), 1)

        def scores(hh, k0, width, mask):
            s = _dot_f32(qa[hh], ka_scr[hh, pl.ds(k0, width), :], NT)
            return s if mask is None else jnp.where(mask, s, NEG_INF)

        def step(k0, width, mask):
            for hh in range(FOX_HEADS):
                s = scores(hh, k0, width, mask)
                m_old = r_scr[hh, :rows]
                blk = s[:, :LANES]
                for t in range(1, width // LANES):
                    blk = jnp.maximum(blk, s[:, LANES * t:LANES * (t + 1)])
                m_new = jnp.maximum(m_old, jnp.broadcast_to(jnp.max(blk, axis=-1, keepdims=True), (rows, LANES)))
                p = jnp.exp2(s - jnp.concatenate([m_new] * (width // LANES), axis=1))
                acc_scr[hh, :rows] = jnp.exp2(m_old - m_new) * acc_scr[hh, :rows] + _dot_f32(
                    p.astype(BF16), va_scr[hh, pl.ds(k0, width), :])
                r_scr[hh, :rows] = m_new

        def body(j, carry):
            step(pl.multiple_of(j * kc, kc), kc, None)
            return carry

        r_scr[...] = jnp.full(r_scr.shape, NEG_INF, F32)
        acc_scr[...] = jnp.zeros_like(acc_scr)
        lax.fori_loop(0, n_chunks, body, 0)
        step(tail_k0, tail_w, tail_mask)
        outs = [acc_scr[h, :rows] / pltpu.roll(acc_scr[h, :rows], FOX_DH, 1) for h in range(FOX_HEADS)]
        for pr in range(FOX_HEADS // 2):
            o = jnp.where(lane_q < FOX_DH, outs[2 * pr], outs[2 * pr + 1])
            o_ref[0, pl.ds(q0, rows), LANES * pr:LANES * (pr + 1)] = o.astype(o_ref.dtype)

    def even_block(u, rows=bq):
        q0 = pl.multiple_of(u * kc, kc) if not isinstance(u, int) else u * kc
        qblock(q0, u, q0, bq, rows)

    def odd_block(u, rows=bq):
        k0 = pl.multiple_of(u * kc, kc) if not isinstance(u, int) else u * kc
        qblock(k0 + bq, u, k0, kc, rows)

    def pair_body(u, carry):
        even_block(u)
        odd_block(u)
        return carry

    n_whole = seq // bq
    lax.fori_loop(0, n_whole // 2, pair_body, 0)
    rest = [(i, bq) for i in range(n_whole - n_whole % 2, n_whole)]
    if n_whole < nq:
        rest.append((n_whole, seq - n_whole * bq))
    for i, rows in rest:
        (odd_block if i % 2 else even_block)(i // 2, rows)


def _fox_attn(z3, c):
    nb, seq, _ = z3.shape
    bq = 256
    padded = -(-seq // bq) * bq

    def zspec(c0):
        return pl.BlockSpec((1, seq, FOX_W), lambda b, c0=c0: (b, 0, c0 // FOX_W))

    return pl.pallas_call(
        functools.partial(_fox_attn_kernel, seq=seq, bq=bq),
        grid=(nb,),
        in_specs=[zspec(C_FQ), zspec(C_FK), zspec(C_FV),
                  pl.BlockSpec((1, seq, LANES), lambda b: (b, 0, 0))],
        out_specs=pl.BlockSpec((1, seq, FOX_W), lambda b: (b, 0, 0)),
        out_shape=jax.ShapeDtypeStruct((nb, seq, FOX_W), BF16),
        scratch_shapes=[pltpu.VMEM((FOX_HEADS, padded, LANES), BF16)] * 3
        + [pltpu.VMEM((FOX_HEADS, bq, LANES), F32)] * 2,
        compiler_params=_cparams("parallel"),
        name="fox_attn",
    )(z3, z3, z3, c)


def _kv_out_kernel(fk_ref, fv_ref, k_hbm, v_hbm, kt_ref, vt_ref, tail_scr, *, seq):
    del k_hbm, v_hbm
    n_full, tail = divmod(seq, LANES)
    for src, dst in ((fk_ref, kt_ref), (fv_ref, vt_ref)):
        for i in range(n_full):
            dst[:, LANES * i:LANES * (i + 1)] = src[0, LANES * i:LANES * (i + 1), :].T
        if tail:
            tail_scr[...] = _pad_rows(src[0, n_full * LANES:seq, :], LANES).T
            dst[:, n_full * LANES:seq] = tail_scr[:, :tail]


def _kv_out(z3, kt_all, vt_all, layer):
    nb, seq, _ = z3.shape
    out_spec = pl.BlockSpec((None, None, FOX_W, seq), lambda b: (layer, b, 0, 0))
    return pl.pallas_call(
        functools.partial(_kv_out_kernel, seq=seq),
        grid=(nb,),
        in_specs=[pl.BlockSpec((1, seq, FOX_W), lambda b: (b, 0, C_FK // FOX_W)),
                  pl.BlockSpec((1, seq, FOX_W), lambda b: (b, 0, C_FV // FOX_W)),
                  pl.BlockSpec(memory_space=pl.ANY), pl.BlockSpec(memory_space=pl.ANY)],
        out_specs=[out_spec, out_spec],
        out_shape=[jax.ShapeDtypeStruct(kt_all.shape, F32), jax.ShapeDtypeStruct(vt_all.shape, F32)],
        input_output_aliases={2: 0, 3: 1},
        scratch_shapes=[pltpu.VMEM((FOX_W, LANES), F32)],
        compiler_params=_cparams("parallel"),
        name="kv_out",
    )(z3, z3, kt_all, vt_all)


def _fox_decode_kernel(pt_ref, fq_ref, fk_ref, fv_ref, ff_ref, *rest, n_pages, page, seq):
    del pt_ref
    k_refs, v_refs, lf_refs = rest[:n_pages], rest[n_pages:2 * n_pages], rest[2 * n_pages:3 * n_pages]
    o_ref, lfo_ref = rest[3 * n_pages:]
    assert page == LANES
    rows = FOX_HEADS * seq
    lfn = _log_sigmoid(ff_ref[...])
    lfo_ref[...] = lfn
    lfn = jnp.where(_iota(lfn.shape, 1) < FOX_HEADS, lfn, 0.0)
    lfn_t = _pad_rows(lfn, LANES).T[:FOX_HEADS]
    stack = jnp.concatenate([r[0, 0] for r in lf_refs] + [lfn_t], axis=0)
    ustrict = jnp.where(_iota((LANES, LANES), 0) > _iota((LANES, LANES), 1), 1.0, 0.0).astype(BF16)
    within = _mask_dot_right(stack, ustrict)
    tot = jnp.sum(stack, axis=-1, keepdims=True)
    g = [None] * (n_pages + 1)
    run = jnp.zeros((FOX_HEADS, 1), F32)
    for p in reversed(range(n_pages + 1)):
        rs = slice(FOX_HEADS * p, FOX_HEADS * (p + 1))
        g[p] = within[rs] + run
        run = run + tot[rs]
    gq = _pad_rows(g[n_pages], LANES).T[:seq]
    lane = _iota((seq, LANES), 1)
    gq_col = jnp.concatenate(
        [jnp.sum(jnp.where(lane == h, gq, 0.0), axis=-1, keepdims=True) for h in range(FOX_HEADS)], axis=0)

    def bias(gp):
        return jnp.concatenate(
            [jnp.broadcast_to(gp[h:h + 1], (seq, page)) for h in range(FOX_HEADS)], axis=0) - gq_col

    own = _iota((rows, FOX_W), 0) // seq == _iota((rows, FOX_W), 1) // FOX_DH
    q = fq_ref[...] * (FOX_DH ** -0.5)
    q_bd = jnp.where(own, jnp.concatenate([q] * FOX_HEADS, axis=0), 0.0).astype(BF16)
    s_blocks = []
    for p in range(n_pages):
        kt = k_refs[p][0, 0].reshape(FOX_W, page).astype(BF16)
        s_blocks.append(_dot_f32(q_bd, kt) + bias(g[p]))
    s = _dot_f32(q_bd, _pad_rows(fk_ref[...], page).astype(BF16), NT) + bias(g[n_pages])
    s_blocks.append(jnp.where(_iota((rows, page), 1) <= _iota((rows, page), 0) % seq, s, NEG_INF))
    mx = s_blocks[0]
    for s in s_blocks[1:]:
        mx = jnp.maximum(mx, s)
    m = jnp.max(mx, axis=-1, keepdims=True)
    wsum = jnp.zeros((rows, page), F32)
    acc = jnp.zeros((rows, FOX_W), F32)
    for p in range(n_pages + 1):
        pe = jnp.exp(s_blocks[p] - m)
        wsum = wsum + pe
        if p < n_pages:
            acc = acc + _dot_f32(pe.astype(BF16), v_refs[p][0, 0].reshape(FOX_W, page).astype(BF16), NT)
        else:
            acc = acc + _dot_f32(pe.astype(BF16), _pad_rows(fv_ref[...], page).astype(BF16))
    acc = jnp.where(own, acc / jnp.sum(wsum, axis=-1, keepdims=True), 0.0)
    o = acc[:seq]
    for h in range(1, FOX_HEADS):
        o = o + acc[seq * h:seq * (h + 1)]
    o_ref[...] = o


def _fox_decode(z, cache_kt, cache_vt, cache_lft, page_table, layer, seq):
    n = z.shape[0]
    nb, n_pages = page_table.shape
    page = cache_lft.shape[-1]
    assert nb * seq == n and seq == 8

    def zspec(c0, w):
        return pl.BlockSpec((seq, w), lambda b, pt, c0=c0, w=w: (b, c0 // w))

    def pspec(shape, p):
        return pl.BlockSpec(shape, lambda b, pt, p=p: (layer, pt[b, p]) + (0,) * (len(shape) - 2))

    kv_shape = (1, 1, FOX_HEADS, FOX_DH, page)
    lf_shape = (1, 1, FOX_HEADS, page)
    in_specs = [zspec(C_FQ, FOX_W), zspec(C_FK, FOX_W), zspec(C_FV, FOX_W), zspec(C_FF, LANES)]
    in_specs += [pspec(kv_shape, p) for p in range(n_pages)]
    in_specs += [pspec(kv_shape, p) for p in range(n_pages)]
    in_specs += [pspec(lf_shape, p) for p in range(n_pages)]
    grid_spec = pltpu.PrefetchScalarGridSpec(
        num_scalar_prefetch=1,
        grid=(nb,),
        in_specs=in_specs,
        out_specs=[pl.BlockSpec((seq, FOX_W), lambda b, pt: (b, 0)),
                   pl.BlockSpec((seq, LANES), lambda b, pt: (b, 0))],
    )
    return pl.pallas_call(
        functools.partial(_fox_decode_kernel, n_pages=n_pages, page=page, seq=seq),
        grid_spec=grid_spec,
        out_shape=[jax.ShapeDtypeStruct((n, FOX_W), F32),
                   jax.ShapeDtypeStruct((n, LANES), F32)],
        compiler_params=_cparams("arbitrary"),
        name="fox_decode",
    )(page_table, z, z, z, z, *([cache_kt] * n_pages), *([cache_vt] * n_pages), *([cache_lft] * n_pages))


def _mix_kernel(x_ref, ya_ref, yb_ref, ga_ref, gb_ref, woa_ref, wob_ref, wo_ref, o_ref):
    ya = _dot_f32(ya_ref[...].astype(BF16), woa_ref[...])
    yb = _dot_f32(yb_ref[...].astype(BF16), wob_ref[...])
    merged = _sigmoid(ga_ref[...]) * ya + _sigmoid(gb_ref[...]) * yb
    o_ref[...] = x_ref[...] + _dot_f32(merged.astype(BF16), wo_ref[...])


def _mix(x, ya, yb, z, woa, wob, wo):
    n, d = x.shape
    assert d == C_GB - C_GA
    tm = _pick_tile(n, 704)
    row = lambda w: pl.BlockSpec((tm, w), lambda i: (i, 0))
    full = lambda a: pl.BlockSpec(a.shape, lambda i: (0, 0))
    return pl.pallas_call(
        _mix_kernel,
        grid=(n // tm,),
        in_specs=[row(d), row(HG_W), row(FOX_W),
                  pl.BlockSpec((tm, d), lambda i: (i, C_GA // d)),
                  pl.BlockSpec((tm, d), lambda i: (i, C_GB // d)),
                  full(woa), full(wob), full(wo)],
        out_specs=row(d),
        out_shape=jax.ShapeDtypeStruct((n, d), F32),
        compiler_params=_cparams("parallel"),
        name="mix_out",
    )(x, ya, yb, z, z, woa, wob, wo)


def _ffn_kernel(x_ref, g_ref, wa_ref, wg_ref, cw_ref, cb_ref, wd_ref, ext_ref, o_ref, at_ref,
                h_scr, carry_scr, *, seq, tiles_per_seq):
    i, j = pl.program_id(0), pl.program_id(1)
    tm = x_ref.shape[0]

    @pl.when(j == 0)
    def _():
        x = x_ref[...]
        h_scr[...] = _rmsnorm(x, g_ref[...]).astype(BF16)
        o_ref[...] = x

    h = h_scr[...]
    a = _dot_f32(h, wa_ref[...])
    gate = _dot_f32(h, wg_ref[...])
    row = _iota(a.shape, 0)
    r1 = pltpu.roll(a, 1, 0)
    r2 = pltpu.roll(a, 2, 0)
    if tiles_per_seq:
        @pl.when((i == 0) & (j == 0))
        def _():
            carry_scr[...] = jnp.zeros_like(carry_scr)

        prev = carry_scr[j]
        seq_start = i % tiles_per_seq == 0
        p1 = jnp.where(seq_start, 0.0, prev[7:8])
        p2 = jnp.where(seq_start, 0.0, prev[6:7])
        a1 = jnp.where(row == 0, p1, r1)
        a2 = jnp.where(row == 0, p2, jnp.where(row == 1, p1, r2))
        carry_scr[j] = a[tm - 8:tm]
    else:
        ext = ext_ref[...]
        pos = row % seq
        a1 = jnp.where(pos == 0, pltpu.roll(ext, tm - 1, 0), r1)
        a2 = jnp.where(pos < 2, ext, r2)
    cw = cw_ref[...]
    conv = cb_ref[...] + cw[0:1] * a2 + cw[1:2] * a1 + cw[2:3] * a
    act = conv * _sigmoid(conv) * gate
    o_ref[...] += _dot_f32(act.astype(BF16), wd_ref[...])
    at_ref[...] = a[tm - at_ref.shape[0]:]


def _ffn(x, g, w_up, conv_w, conv_b, w_down, ext, seq):
    n, d = x.shape
    d_ff = w_down.shape[0]
    tf = 256
    assert d_ff % tf == 0
    nj = d_ff // tf
    if ext is None:
        tm = _pick_tile(seq, 1040)
        tiles_per_seq = seq // tm
        out_rows = 8
        ext = jnp.zeros((8, d_ff), F32)
        ext_spec = pl.BlockSpec((8, tf), lambda i, j: (0, j))
    else:
        tm = _pick_tile(n, 1040)
        assert tm % seq == 0 and seq == 8
        tiles_per_seq = 0
        out_rows = tm
        ext_spec = pl.BlockSpec((tm, tf), lambda i, j: (i, j))
    nt = n // tm
    out, a_tail = pl.pallas_call(
        functools.partial(_ffn_kernel, seq=seq, tiles_per_seq=tiles_per_seq),
        grid=(nt, nj),
        in_specs=[
            pl.BlockSpec((tm, d), lambda i, j: (i, 0)),
            pl.BlockSpec((1, d), lambda i, j: (0, 0)),
            pl.BlockSpec((d, tf), lambda i, j: (0, j)),
            pl.BlockSpec((d, tf), lambda i, j: (0, j + nj)),
            pl.BlockSpec((CONV_W, tf), lambda i, j: (0, j)),
            pl.BlockSpec((1, tf), lambda i, j: (0, j)),
            pl.BlockSpec((tf, d), lambda i, j: (j, 0)),
            ext_spec,
        ],
        out_specs=[pl.BlockSpec((tm, d), lambda i, j: (i, 0)),
                   pl.BlockSpec((out_rows, tf), lambda i, j: (i, j))],
        out_shape=[jax.ShapeDtypeStruct((n, d), F32),
                   jax.ShapeDtypeStruct((nt * out_rows, d_ff), F32)],
        scratch_shapes=[pltpu.VMEM((tm, d), BF16), pltpu.VMEM((nj, 8, tf), F32)],
        compiler_params=_cparams("arbitrary", "arbitrary"),
        name="conv_ffn",
    )(x, g.reshape(1, d), w_up, w_up, conv_w, conv_b.reshape(1, d_ff), w_down, ext)
    return out, a_tail, tiles_per_seq


def _norm_kernel(x_ref, g_ref, o_ref):
    o_ref[...] = _rmsnorm(x_ref[...], g_ref[...])


def _final_norm(x, g):
    n, d = x.shape
    tm = _pick_tile(n, 1040)
    return pl.pallas_call(
        _norm_kernel,
        grid=(n // tm,),
        in_specs=[pl.BlockSpec((tm, d), lambda i: (i, 0)), pl.BlockSpec((1, d), lambda i: (0, 0))],
        out_specs=pl.BlockSpec((tm, d), lambda i: (i, 0)),
        out_shape=jax.ShapeDtypeStruct((n, d), F32),
        compiler_params=_cparams("parallel"),
        name="final_norm",
    )(x, g.reshape(1, d))


def _reorder_in_proj(w_in, b_in):
    n_hf = 4 * HG_W + 3 * FOX_W
    pad = LANES - FOX_HEADS

    def reorder(a):
        parts = [a[..., n_hf + FOX_HEADS:], a[..., :n_hf], a[..., n_hf:n_hf + FOX_HEADS],
                 jnp.zeros(a.shape[:-1] + (pad,), a.dtype)]
        return jnp.concatenate(parts, axis=-1)

    return reorder(w_in).astype(BF16), reorder(b_in)


def kernel(x_prompt, x_sample, cache_k, cache_v, cache_logf, state_hgrn, state_conv, page_table, meta_tokens, norm1, norm2, norm_f, w_in, b_in, hg_lower_bounds, hg_norm, w_oa, w_ob, w_o, w_up, conv_w, conv_b, w_down):
    nb, s_len, d = x_prompt.shape
    seq_p = s_len + N_META
    db, seq_s, _ = x_sample.shape
    depth = w_in.shape[0]
    d_ff = w_down.shape[1]
    n_pool, page = cache_k.shape[1], cache_k.shape[2]
    assert w_in.shape[2] == 4 * HG_W + 3 * FOX_W + FOX_HEADS + 2 * d and 2 * d == C_ZQ

    w_in_r, b_in_r = _reorder_in_proj(w_in, b_in)
    w_oa, w_ob, w_o, w_up, w_down = (w.astype(BF16) for w in (w_oa, w_ob, w_o, w_up, w_down))
    cache_kt = jnp.transpose(cache_k, (0, 1, 3, 4, 2))
    cache_vt = jnp.transpose(cache_v, (0, 1, 3, 4, 2))
    cache_lft = jnp.swapaxes(cache_logf, 2, 3)
    lbraw = hg_lower_bounds.astype(F32)

    meta = jnp.broadcast_to(meta_tokens.astype(x_prompt.dtype)[None], (nb, N_META, d))
    xp = jnp.concatenate([meta, x_prompt], axis=1).reshape(nb * seq_p, d)
    xs = x_sample.reshape(db * seq_s, d)

    outs = [[] for _ in range(6)]
    k_p, v_p = (jnp.zeros((depth, nb, FOX_W, seq_p), F32) for _ in range(2))
    k_s, v_s = (jnp.zeros((depth, db * seq_s, FOX_W), F32) for _ in range(2))
    for l in range(depth):
        z = _inproj(xp, norm1[l], w_in_r[l], b_in_r[l])
        z3 = z.reshape(nb, seq_p, NZ)
        k_p, v_p = _kv_out(z3, k_p, v_p, l)
        y_hg, s_new = _hgrn_prompt(z3, lbraw, hg_norm[l], l)
        lf, c = _fox_gate(z3)
        o_fox = _fox_attn(z3, c)
        xp = _mix(xp, y_hg.reshape(nb * seq_p, HG_W), o_fox.reshape(nb * seq_p, FOX_W), z, w_oa[l], w_ob[l], w_o[l])
        xp, a_tail, tps = _ffn(xp, norm2[l], w_up[l], conv_w[l], conv_b[l], w_down[l], None, seq_p)
        outs[0].append(lf[:, :, :FOX_HEADS])
        outs[1].append(s_new)
        outs[2].append(a_tail.reshape(nb, tps, 8, d_ff)[:, tps - 1, 8 - (CONV_W - 1):])
        z, k_s, v_s = _inproj(xs, norm1[l], w_in_r[l], b_in_r[l], (k_s, v_s), l)
        y_hg, s_new = _hgrn_sample(z, state_hgrn, lbraw, hg_norm[l], l, seq_s)
        o_fox, lf = _fox_decode(z, cache_kt, cache_vt, cache_lft, page_table, l, seq_s)
        xs = _mix(xs, y_hg, o_fox, z, w_oa[l], w_ob[l], w_o[l])
        ext = jnp.pad(state_conv[l], ((0, 0), (0, seq_s - (CONV_W - 1)), (0, 0))).reshape(db * seq_s, d_ff)
        xs, a_all, _ = _ffn(xs, norm2[l], w_up[l], conv_w[l], conv_b[l], w_down[l], ext, seq_s)
        outs[3].append(lf.reshape(db, seq_s, LANES)[:, :, :FOX_HEADS])
        outs[4].append(s_new)
        outs[5].append(a_all.reshape(db, seq_s, d_ff)[:, seq_s - (CONV_W - 1):])
    y_prompt = _final_norm(xp, norm_f).reshape(nb, seq_p, d)[:, N_META:]
    y_sample = _final_norm(xs, norm_f).reshape(db, seq_s, d)
    lf_p, hg_p, cv_p, lf_s, hg_s, cv_s = (jnp.stack(o) for o in outs)
    k_p, v_p = (jnp.transpose(a.reshape(depth, nb, FOX_HEADS, FOX_DH, seq_p), (0, 1, 4, 2, 3)) for a in (k_p, v_p))
    kv_s = (depth, db, seq_s, FOX_HEADS, FOX_DH)
    return (y_prompt, y_sample, k_p, v_p, lf_p, hg_p, cv_p,
            k_s.reshape(kv_s), v_s.reshape(kv_s), lf_s, hg_s, cv_s)
```

```python
import functools

import jax
import jax.numpy as jnp
from jax import lax
from jax.experimental import pallas as pl
from jax.experimental.pallas import tpu as pltpu

F32 = jnp.float32
BF16 = jnp.bfloat16
EPS = 1e-6
NEG_INF = float("-inf")
LOG2E = 1.4426950408889634

N_META = 16
HG_HEADS = 4
HG_DK = 128
FOX_HEADS = 8
FOX_DH = 64
CONV_W = 3

LANES = 128
HG_EXACT = 8
HG_GROUP = 4
VMEM_LIMIT = 56 * 1024 * 1024

HG_W = HG_HEADS * HG_DK
FOX_W = FOX_HEADS * FOX_DH
C_GA = 0
C_GB = 1024
C_ZQ = 2048
C_ZF = C_ZQ + HG_W
C_ZI = C_ZF + HG_W
C_ZG = C_ZI + HG_W
C_FQ = C_ZG + HG_W
C_FK = C_FQ + FOX_W
C_FV = C_FK + FOX_W
C_FF = C_FV + FOX_W
NZ = C_FF + LANES


def _cparams(*sem):
    return pltpu.CompilerParams(dimension_semantics=sem, vmem_limit_bytes=VMEM_LIMIT)


def _pick_tile(n, cap):
    best = None
    for d in range(8, min(n, cap) + 1, 8):
        if n % d == 0:
            best = d
    assert best is not None, (n, cap)
    return best


def _sigmoid(x):
    return 0.5 * jnp.tanh(0.5 * x) + 0.5


def _log_sigmoid(x):
    return jnp.minimum(x, 0.0) - jnp.log1p(jnp.exp(-jnp.abs(x)))


def _split3(x):
    hi = x.astype(BF16)
    r = x - hi.astype(F32)
    mid = r.astype(BF16)
    r = r - mid.astype(F32)
    return hi, mid, r.astype(BF16)


def _dot_f32(a, b, dims=None):
    if dims is None:
        return jnp.dot(a, b, preferred_element_type=F32)
    return lax.dot_general(a, b, (dims, ((), ())), preferred_element_type=F32)


NT = ((1,), (1,))


def _mask_dot_left(m, x):
    return sum(_dot_f32(m, p) for p in _split3(x))


def _mask_dot_right(x, m):
    return sum(_dot_f32(p, m) for p in _split3(x))


def _rmsnorm(x, g):
    return x * lax.rsqrt(jnp.mean(x * x, axis=-1, keepdims=True) + EPS) * g


def _iota(shape, dim):
    return lax.broadcasted_iota(jnp.int32, shape, dim)


def _inproj_kernel(x_ref, g_ref, w_ref, b_ref, *rest, tn, with_kv):
    if with_kv:
        z_ref, k_ref, v_ref, h_scr = rest[-4:]
    else:
        z_ref, h_scr = rest
    j = pl.program_id(1)

    @pl.when(j == 0)
    def _():
        h_scr[...] = _rmsnorm(x_ref[...], g_ref[...]).astype(BF16)

    z = _dot_f32(h_scr[...], w_ref[...]) + b_ref[...]
    z_ref[...] = z

    if with_kv:
        @pl.when(j == C_FK // tn)
        def _():
            k_ref[...] = z[:, C_FK % tn:C_FK % tn + FOX_W]
            v_ref[...] = z[:, C_FV % tn:C_FV % tn + FOX_W]


def _inproj(x, g, w, b, kv_depth=0, kv=None, layer=0):
    n, d = x.shape
    tm = _pick_tile(n, 1040)
    tn = 1152
    assert NZ % tn == 0 and C_FK // tn == (C_FV + FOX_W - 1) // tn
    in_specs = [
        pl.BlockSpec((tm, d), lambda i, j: (i, 0)),
        pl.BlockSpec((1, d), lambda i, j: (0, 0)),
        pl.BlockSpec((d, tn), lambda i, j: (0, j)),
        pl.BlockSpec((1, tn), lambda i, j: (0, j)),
    ]
    out_specs = [pl.BlockSpec((tm, tn), lambda i, j: (i, j))]
    out_shape = [jax.ShapeDtypeStruct((n, NZ), F32)]
    args = [x, g.reshape(1, d), w, b.reshape(1, NZ)]
    aliases = {}
    if kv_depth:
        out_specs += [pl.BlockSpec((None, tm, FOX_W), lambda i, j: (layer, i, 0))] * 2
        out_shape += [jax.ShapeDtypeStruct((kv_depth, n, FOX_W), F32)] * 2
    if kv is not None:
        in_specs += [pl.BlockSpec(memory_space=pl.ANY)] * 2
        args += list(kv)
        aliases = {4: 1, 5: 2}
    out = pl.pallas_call(
        functools.partial(_inproj_kernel, tn=tn, with_kv=bool(kv_depth)),
        grid=(n // tm, NZ // tn),
        in_specs=in_specs,
        out_specs=out_specs,
        out_shape=out_shape,
        input_output_aliases=aliases,
        scratch_shapes=[pltpu.VMEM((tm, d), BF16)],
        compiler_params=_cparams("parallel", "arbitrary"),
        name="inproj",
    )(*args)
    return out if kv_depth else out[0]


def _hgrn_lower_bound(lbraw, layer):
    e = jnp.exp(lbraw - jnp.max(lbraw, axis=0, keepdims=True))
    sm = e / jnp.sum(e, axis=0, keepdims=True)
    lb = jnp.zeros_like(sm[0:1])
    for i in range(1, layer + 1):
        lb = lb + sm[i:i + 1]
    return lb


def _hgrn_gates(zq, zf, zi, lb):
    q = zq * _sigmoid(zq)
    a = jnp.log(lb)
    b = jnp.log1p(-lb) + _log_sigmoid(zf)
    logf = jnp.maximum(a, b) + jnp.log1p(jnp.exp(-jnp.abs(a - b)))
    k = (1.0 - lb) * _sigmoid(-zf)
    return q, k, zi, logf


def _hgrn_diag(q, k, b, v):
    n = q.shape[0]
    rowid = _iota(q.shape, 0)
    acc = jnp.zeros_like(v)
    for s in range(n):
        d = jnp.where(rowid >= s, b - b[s:s + 1], NEG_INF)
        w = q * k[s:s + 1] * jnp.exp2(d)
        acc = acc + jnp.sum(w, axis=-1, keepdims=True) * v[s:s + 1]
    return acc


def _hgrn_head_out(o, zg, gain):
    o = o * lax.rsqrt(jnp.mean(o * o, axis=-1, keepdims=True) + EPS)
    return o * gain * (zg * _sigmoid(zg))


def _pad_rows(x, rows):
    if x.shape[0] == rows:
        return x
    return jnp.concatenate([x, jnp.zeros((rows - x.shape[0], x.shape[1]), x.dtype)], axis=0)


def _hgrn_prompt_kernel(zq_ref, zf_ref, zi_ref, zg_ref, lbraw_ref, gain_ref, y_ref, s_ref, st_scr,
                        *, layer, seq):
    ck = LANES
    n_full, tail = divmod(seq, ck)
    assert tail % HG_EXACT == 0
    lb_all = _hgrn_lower_bound(lbraw_ref[...], layer)
    gain_all = gain_ref[...]
    row = _iota((ck, ck), 0)
    col = _iota((ck, ck), 1)
    ltri = jnp.where(col <= row, 1.0, 0.0).astype(BF16)
    levels = []
    half = HG_EXACT
    while half < ck:
        levels.append((half, row % (2 * half) >= half, row // (2 * half) == col // (2 * half)))
        half *= 2
    st_scr[...] = jnp.zeros_like(st_scr)

    def chunk(r0, n_rows):
        for g in range(HG_GROUP):
            head_chunk(r0, n_rows, g, slice(HG_DK * g, HG_DK * (g + 1)))

    def head_chunk(r0, n_rows, g, hs):
        sl = pl.ds(r0, n_rows)
        lb, gain = lb_all[:, hs], gain_all[:, hs]
        q, k, v, logf = _hgrn_gates(zq_ref[0, sl, hs], zf_ref[0, sl, hs], zi_ref[0, sl, hs], lb)
        q, k, v, logf = (_pad_rows(a, ck) for a in (q, k, v, logf))
        b = _mask_dot_left(ltri, logf * LOG2E)
        bend = b[ck - 1:ck]
        a = jnp.zeros((ck, ck), F32)
        for half, upper, same_blk in levels:
            blk = 2 * half
            bmid = jnp.concatenate(
                [jnp.broadcast_to(b[blk * i + half - 1:blk * i + half], (blk, ck)) for i in range(ck // blk)],
                axis=0)
            q2 = jnp.where(upper, q * jnp.exp2(jnp.minimum(b - bmid, 0.0)), 0.0).astype(BF16)
            k2 = jnp.where(upper, 0.0, k * jnp.exp2(jnp.minimum(bmid - b, 0.0))).astype(BF16)
            a = a + jnp.where(same_blk, _dot_f32(q2, k2, NT), 0.0)
        st = st_scr[g]
        o = _dot_f32(a.astype(BF16), v.astype(BF16))
        o = o + _dot_f32((q * jnp.exp2(b)).astype(BF16), st.astype(BF16), NT)
        diag = [_hgrn_diag(*(x[HG_EXACT * i:HG_EXACT * (i + 1)] for x in (q, k, b, v)))
                for i in range(n_rows // HG_EXACT)]
        o = o[:n_rows] + jnp.concatenate(diag, axis=0)
        st_scr[g] = st * jnp.exp2(bend) + _dot_f32(v.T.astype(BF16), (k * jnp.exp2(bend - b)).astype(BF16))
        y_ref[0, sl, hs] = _hgrn_head_out(o, zg_ref[0, sl, hs], gain).astype(y_ref.dtype)

    def body(c, carry):
        chunk(pl.multiple_of(c * ck, ck), ck)
        return carry

    lax.fori_loop(0, n_full, body, 0)
    if tail:
        chunk(n_full * ck, tail)
    for g in range(HG_GROUP):
        s_ref[0, g] = st_scr[g].T


def _hgrn_prompt(z3, lbraw, gain, layer):
    nb, seq, _ = z3.shape
    gw = HG_GROUP * HG_DK

    def zspec(c0):
        return pl.BlockSpec((1, seq, gw), lambda b, h, c0=c0: (b, 0, c0 // gw + h))

    return pl.pallas_call(
        functools.partial(_hgrn_prompt_kernel, layer=layer, seq=seq),
        grid=(nb, HG_HEADS // HG_GROUP),
        in_specs=[zspec(C_ZQ), zspec(C_ZF), zspec(C_ZI), zspec(C_ZG),
                  pl.BlockSpec((lbraw.shape[0], gw), lambda b, h: (0, h)),
                  pl.BlockSpec((1, gw), lambda b, h: (0, h))],
        out_specs=[pl.BlockSpec((1, seq, gw), lambda b, h: (b, 0, h)),
                   pl.BlockSpec((1, HG_GROUP, HG_DK, HG_DK), lambda b, h: (b, h, 0, 0))],
        out_shape=[jax.ShapeDtypeStruct((nb, seq, HG_W), BF16),
                   jax.ShapeDtypeStruct((nb, HG_HEADS, HG_DK, HG_DK), F32)],
        scratch_shapes=[pltpu.VMEM((HG_GROUP, HG_DK, HG_DK), F32)],
        compiler_params=_cparams("parallel", "parallel"),
        name="hgrn_prompt",
    )(z3, z3, z3, z3, lbraw, gain.reshape(1, HG_W))


def _hgrn_sample_kernel(zq_ref, zf_ref, zi_ref, zg_ref, lbraw_ref, gain_ref, s0_ref, *rest, layer, seq):
    y_ref, s_ref = rest[-2:]
    rows = zq_ref.shape[0]
    lb_all = _hgrn_lower_bound(lbraw_ref[...], layer)
    row = _iota((rows, rows), 0)
    col = _iota((rows, rows), 1)
    ltri = jnp.where((row // seq == col // seq) & (col <= row), 1.0, 0.0).astype(BF16)
    for g in range(HG_HEADS):
        hs = slice(HG_DK * g, HG_DK * (g + 1))
        _hgrn_sample_head(zq_ref, zf_ref, zi_ref, zg_ref, gain_ref, s0_ref, y_ref, s_ref, lb_all[:, hs], ltri,
                          g, hs, seq)


def _hgrn_sample_head(zq_ref, zf_ref, zi_ref, zg_ref, gain_ref, s0_ref, y_ref, s_ref, lb, ltri, g, hs, seq):
    rows = zq_ref.shape[0]
    n_seq = rows // seq
    q, k, v, logf = _hgrn_gates(zq_ref[:, hs], zf_ref[:, hs], zi_ref[:, hs], lb)
    b = _mask_dot_left(ltri, logf * LOG2E)
    bend = jnp.concatenate(
        [jnp.broadcast_to(b[seq * s + seq - 1:seq * (s + 1)], (seq, LANES)) for s in range(n_seq)], axis=0)
    qt = (q * jnp.exp2(b)).astype(BF16)
    kt = k * jnp.exp2(bend - b)
    outs = []
    for s in range(n_seq):
        rs = slice(seq * s, seq * (s + 1))
        s0 = s0_ref[s, g]
        outs.append(_dot_f32(qt[rs], s0.astype(BF16)) + _hgrn_diag(q[rs], k[rs], b[rs], v[rs]))
        x = jnp.concatenate([kt[rs], jnp.exp2(bend[rs]), jnp.zeros((HG_DK - 2 * seq, LANES), F32)], axis=0)
        xt = x.T
        inc = _dot_f32(xt.astype(BF16), _pad_rows(v[rs], HG_DK).astype(BF16))
        s_ref[s, g] = xt[:, seq:seq + 1] * s0 + inc
    o = jnp.concatenate(outs, axis=0)
    y_ref[:, hs] = _hgrn_head_out(o, zg_ref[:, hs], gain_ref[:, hs]).astype(y_ref.dtype)


def _hgrn_sample(z, state, lbraw, gain, layer, seq, new_state):
    n = z.shape[0]
    rows = LANES
    assert n % rows == 0 and rows % seq == 0 and 2 * seq <= HG_DK
    sblk = rows // seq

    def zspec(c0):
        return pl.BlockSpec((rows, HG_W), lambda i, c0=c0: (i, c0 // HG_W))

    st_spec = pl.BlockSpec((None, sblk, HG_HEADS, HG_DK, HG_DK), lambda i: (layer, i, 0, 0, 0))
    in_specs = [zspec(C_ZQ), zspec(C_ZF), zspec(C_ZI), zspec(C_ZG),
                pl.BlockSpec(lbraw.shape, lambda i: (0, 0)),
                pl.BlockSpec((1, HG_W), lambda i: (0, 0)),
                st_spec]
    args = [z, z, z, z, lbraw, gain.reshape(1, HG_W), state]
    aliases = {}
    if new_state is not None:
        in_specs.append(pl.BlockSpec(memory_space=pl.ANY))
        args.append(new_state)
        aliases = {7: 1}
    return pl.pallas_call(
        functools.partial(_hgrn_sample_kernel, layer=layer, seq=seq),
        grid=(n // rows,),
        in_specs=in_specs,
        out_specs=[pl.BlockSpec((rows, HG_W), lambda i: (i, 0)), st_spec],
        out_shape=[jax.ShapeDtypeStruct((n, HG_W), BF16),
                   jax.ShapeDtypeStruct(state.shape, F32)],
        input_output_aliases=aliases,
        compiler_params=_cparams("parallel"),
        name="hgrn_sample",
    )(*args)


def _fox_gate(ff_ref, lf_ref, c_ref, seq):
    ck = LANES
    n_full, tail = divmod(seq, ck)
    ltri = jnp.where(_iota((ck, ck), 1) <= _iota((ck, ck), 0), 1.0, 0.0).astype(BF16)

    def chunk(r0, n_rows, carry):
        sl = pl.ds(r0, n_rows)
        lf = _log_sigmoid(ff_ref[0, sl, :])
        lf_ref[0, sl, :] = lf
        c = _mask_dot_left(ltri, _pad_rows(lf, ck)) + carry
        c_ref[0, sl, :] = c[:n_rows]
        return c[n_rows - 1:n_rows]

    carry = lax.fori_loop(0, n_full, lambda i, cr: chunk(pl.multiple_of(i * ck, ck), ck, cr),
                          jnp.zeros((1, LANES), F32))
    if tail:
        chunk(n_full * ck, tail, carry)


def _fox_attn_kernel(fq_ref, fk_ref, fv_ref, c_ref, o_ref, qa_scr, ka_scr, va_scr, r_scr, acc_scr,
                     *, seq, bq):
    padded = qa_scr.shape[1]
    nq = padded // bq
    ck = LANES
    n_full, tail = divmod(seq, ck)
    scale = FOX_DH ** -0.5 * LOG2E

    def build(r0, n_rows):
        sl = pl.ds(r0, n_rows)
        cb = c_ref[0, sl, :]
        lane = _iota((n_rows, LANES), 1)
        for h in range(FOX_HEADS):
            hh = h % 2
            ps = slice(LANES * (h // 2), LANES * (h // 2 + 1))
            qf, kf, vf = fq_ref[0, sl, ps], fk_ref[0, sl, ps], fv_ref[0, sl, ps]
            ch = jnp.sum(jnp.where(lane == h, cb, 0.0), axis=-1, keepdims=True) * LOG2E
            hi, mid, lo = (p.astype(F32) for p in _split3(ch))
            own = (lane >= FOX_DH * hh) & (lane < FOX_DH * (hh + 1))
            f0 = FOX_DH * (1 - hh)
            ones_q = (lane >= f0 + 3) & (lane < f0 + 6)
            ones_k = (lane >= f0) & (lane < f0 + 3)
            qa = jnp.where(own, qf * scale,
                           jnp.where(lane == f0, hi, jnp.where(lane == f0 + 1, mid, jnp.where(
                               lane == f0 + 2, lo, jnp.where(ones_q, 1.0, 0.0)))))
            ka = jnp.where(own, kf,
                           jnp.where(ones_k, 1.0, jnp.where(lane == f0 + 3, -hi, jnp.where(
                               lane == f0 + 4, -mid, jnp.where(lane == f0 + 5, -lo, 0.0)))))
            qa_scr[h, sl, :] = qa.astype(BF16)
            ka_scr[h, sl, :] = ka.astype(BF16)
            va_scr[h, sl, :] = jnp.where(own, vf, 1.0).astype(BF16)

    def build_body(i, carry):
        build(pl.multiple_of(i * ck, ck), ck)
        return carry

    lax.fori_loop(0, n_full, build_body, 0)
    if tail:
        build(n_full * ck, tail)
    if padded > seq:
        zpad = jnp.zeros((padded - seq, LANES), BF16)
        for h in range(FOX_HEADS):
            qa_scr[h, seq:padded, :] = zpad
            ka_scr[h, seq:padded, :] = zpad
            va_scr[h, seq:padded, :] = zpad

    kc = 2 * bq

    def qblock(q0, n_chunks, tail_k0, tail_w, rows):
        qa = [qa_scr[h, pl.ds(q0, rows), :] for h in range(FOX_HEADS)]
        tail_mask = _iota((rows, tail_w), 1) <= _iota((rows, tail_w), 0) + (tail_w - bq)
        lane_q = _iota((rows, LANES), 1)

        def scores(hh, k0, width, mask):
            s = _dot_f32(qa[hh], ka_scr[hh, pl.ds(k0, width), :], NT)
            return s if mask is None else jnp.where(mask, s, NEG_INF)

        def step(k0, width, mask):
            for hh in range(FOX_HEADS):
                s = scores(hh, k0, width, mask)
                m_old = r_scr[hh, :rows]
                blk = s[:, :LANES]
                for t in range(1, width // LANES):
                    blk = jnp.maximum(blk, s[:, LANES * t:LANES * (t + 1)])
                m_new = jnp.maximum(m_old, jnp.broadcast_to(jnp.max(blk, axis=-1, keepdims=True), (rows, LANES)))
                p = jnp.exp2(s - jnp.concatenate([m_new] * (width // LANES), axis=1))
                acc_scr[hh, :rows] = jnp.exp2(m_old - m_new) * acc_scr[hh, :rows] + _dot_f32(
                    p.astype(BF16), va_scr[hh, pl.ds(k0, width), :])
                r_scr[hh, :rows] = m_new

        def body(j, carry):
            step(pl.multiple_of(j * kc, kc), kc, None)
            return carry

        r_scr[...] = jnp.full(r_scr.shape, NEG_INF, F32)
        acc_scr[...] = jnp.zeros_like(acc_scr)
        lax.fori_loop(0, n_chunks, body, 0)
        step(tail_k0, tail_w, tail_mask)
        outs = [acc_scr[h, :rows] / pltpu.roll(acc_scr[h, :rows], FOX_DH, 1) for h in range(FOX_HEADS)]
        for pr in range(FOX_HEADS // 2):
            o = jnp.where(lane_q < FOX_DH, outs[2 * pr], outs[2 * pr + 1])
            o_ref[0, pl.ds(q0, rows), LANES * pr:LANES * (pr + 1)] = o.astype(o_ref.dtype)

    def even_block(u, rows=bq):
        q0 = pl.multiple_of(u * kc, kc) if not isinstance(u, int) else u * kc
        qblock(q0, u, q0, bq, rows)

    def odd_block(u, rows=bq):
        k0 = pl.multiple_of(u * kc, kc) if not isinstance(u, int) else u * kc
        qblock(k0 + bq, u, k0, kc, rows)

    def pair_body(u, carry):
        even_block(u)
        odd_block(u)
        return carry

    n_whole = seq // bq
    lax.fori_loop(0, n_whole // 2, pair_body, 0)
    rest = [(i, bq) for i in range(n_whole - n_whole % 2, n_whole)]
    if n_whole < nq:
        rest.append((n_whole, seq - n_whole * bq))
    for i, rows in rest:
        (odd_block if i % 2 else even_block)(i // 2, rows)


def _fox_attn(z3, c):
    nb, seq, _ = z3.shape
    bq = 256
    padded = -(-seq // bq) * bq

    def zspec(c0):
        return pl.BlockSpec((1, seq, FOX_W), lambda b, c0=c0: (b, 0, c0 // FOX_W))

    return pl.pallas_call(
        functools.partial(_fox_attn_kernel, seq=seq, bq=bq),
        grid=(nb,),
        in_specs=[zspec(C_FQ), zspec(C_FK), zspec(C_FV),
                  pl.BlockSpec((1, seq, LANES), lambda b: (b, 0, 0))],
        out_specs=pl.BlockSpec((1, seq, FOX_W), lambda b: (b, 0, 0)),
        out_shape=jax.ShapeDtypeStruct((nb, seq, FOX_W), BF16),
        scratch_shapes=[pltpu.VMEM((FOX_HEADS, padded, LANES), BF16)] * 3
        + [pltpu.VMEM((FOX_HEADS, bq, LANES), F32)] * 2,
        compiler_params=_cparams("parallel"),
        name="fox_attn",
    )(z3, z3, z3, c)


def _fox_prep_kernel(fk_ref, fv_ref, ff_ref, *rest, seq):
    kt_ref, vt_ref, lf_ref, c_ref, tail_scr = rest[-5:]
    _fox_gate(ff_ref, lf_ref, c_ref, seq)
    n_full, tail = divmod(seq, LANES)
    for src, dst in ((fk_ref, kt_ref), (fv_ref, vt_ref)):
        for i in range(n_full):
            dst[:, LANES * i:LANES * (i + 1)] = src[0, LANES * i:LANES * (i + 1), :].T
        if tail:
            tail_scr[...] = _pad_rows(src[0, n_full * LANES:seq, :], LANES).T
            dst[:, n_full * LANES:seq] = tail_scr[:, :tail]


def _fox_prep(z3, stacks, layer, depth):
    nb, seq, _ = z3.shape
    kv_spec = pl.BlockSpec((None, None, FOX_W, seq), lambda b: (layer, b, 0, 0))
    g_spec = pl.BlockSpec((1, seq, LANES), lambda b: (b, 0, 0))
    in_specs = [pl.BlockSpec((1, seq, FOX_W), lambda b: (b, 0, C_FK // FOX_W)),
                pl.BlockSpec((1, seq, FOX_W), lambda b: (b, 0, C_FV // FOX_W)),
                pl.BlockSpec((1, seq, LANES), lambda b: (b, 0, C_FF // LANES))]
    args = [z3, z3, z3]
    aliases = {}
    if stacks is not None:
        in_specs += [pl.BlockSpec(memory_space=pl.ANY)] * 2
        args += list(stacks)
        aliases = {3: 0, 4: 1}
    kv_shape = jax.ShapeDtypeStruct((depth, nb, FOX_W, seq), F32)
    g_shape = jax.ShapeDtypeStruct((nb, seq, LANES), F32)
    return pl.pallas_call(
        functools.partial(_fox_prep_kernel, seq=seq),
        grid=(nb,),
        in_specs=in_specs,
        out_specs=[kv_spec, kv_spec, g_spec, g_spec],
        out_shape=[kv_shape, kv_shape, g_shape, g_shape],
        input_output_aliases=aliases,
        scratch_shapes=[pltpu.VMEM((FOX_W, LANES), F32)],
        compiler_params=_cparams("parallel"),
        name="fox_prep",
    )(*args)


def _fox_decode_kernel(pt_ref, fq_ref, fk_ref, fv_ref, ff_ref, *rest, n_pages, page, seq):
    del pt_ref
    k_refs, v_refs, lf_refs = rest[:n_pages], rest[n_pages:2 * n_pages], rest[2 * n_pages:3 * n_pages]
    o_ref, lfo_ref = rest[3 * n_pages:]
    assert page == LANES
    rows = FOX_HEADS * seq
    lfn = _log_sigmoid(ff_ref[...])
    lfo_ref[...] = lfn
    lfn = jnp.where(_iota(lfn.shape, 1) < FOX_HEADS, lfn, 0.0)
    lfn_t = _pad_rows(lfn, LANES).T[:FOX_HEADS]
    stack = jnp.concatenate([r[0, 0] for r in lf_refs] + [lfn_t], axis=0)
    ustrict = jnp.where(_iota((LANES, LANES), 0) > _iota((LANES, LANES), 1), 1.0, 0.0).astype(BF16)
    within = _mask_dot_right(stack, ustrict)
    tot = jnp.sum(stack, axis=-1, keepdims=True)
    g = [None] * (n_pages + 1)
    run = jnp.zeros((FOX_HEADS, 1), F32)
    for p in reversed(range(n_pages + 1)):
        rs = slice(FOX_HEADS * p, FOX_HEADS * (p + 1))
        g[p] = within[rs] + run
        run = run + tot[rs]
    gq = _pad_rows(g[n_pages], LANES).T[:seq]
    lane = _iota((seq, LANES), 1)
    gq_col = jnp.concatenate(
        [jnp.sum(jnp.where(lane == h, gq, 0.0), axis=-1, keepdims=True) for h in range(FOX_HEADS)], axis=0)

    def bias(gp):
        return jnp.concatenate(
            [jnp.broadcast_to(gp[h:h + 1], (seq, page)) for h in range(FOX_HEADS)], axis=0) - gq_col

    own = _iota((rows, FOX_W), 0) // seq == _iota((rows, FOX_W), 1) // FOX_DH
    q = fq_ref[...] * (FOX_DH ** -0.5)
    q_bd = jnp.where(own, jnp.concatenate([q] * FOX_HEADS, axis=0), 0.0).astype(BF16)
    s_blocks = []
    for p in range(n_pages):
        kt = k_refs[p][0, 0].reshape(FOX_W, page).astype(BF16)
        s_blocks.append(_dot_f32(q_bd, kt) + bias(g[p]))
    s = _dot_f32(q_bd, _pad_rows(fk_ref[...], page).astype(BF16), NT) + bias(g[n_pages])
    s_blocks.append(jnp.where(_iota((rows, page), 1) <= _iota((rows, page), 0) % seq, s, NEG_INF))
    mx = s_blocks[0]
    for s in s_blocks[1:]:
        mx = jnp.maximum(mx, s)
    m = jnp.max(mx, axis=-1, keepdims=True)
    wsum = jnp.zeros((rows, page), F32)
    acc = jnp.zeros((rows, FOX_W), F32)
    for p in range(n_pages + 1):
        pe = jnp.exp(s_blocks[p] - m)
        wsum = wsum + pe
        if p < n_pages:
            acc = acc + _dot_f32(pe.astype(BF16), v_refs[p][0, 0].reshape(FOX_W, page).astype(BF16), NT)
        else:
            acc = acc + _dot_f32(pe.astype(BF16), _pad_rows(fv_ref[...], page).astype(BF16))
    acc = jnp.where(own, acc / jnp.sum(wsum, axis=-1, keepdims=True), 0.0)
    o = acc[:seq]
    for h in range(1, FOX_HEADS):
        o = o + acc[seq * h:seq * (h + 1)]
    o_ref[...] = o


def _fox_decode(z, cache_kt, cache_vt, cache_lft, page_table, layer, seq):
    n = z.shape[0]
    nb, n_pages = page_table.shape
    page = cache_lft.shape[-1]
    assert nb * seq == n and seq == 8

    def zspec(c0, w):
        return pl.BlockSpec((seq, w), lambda b, pt, c0=c0, w=w: (b, c0 // w))

    def pspec(shape, p):
        return pl.BlockSpec(shape, lambda b, pt, p=p: (layer, pt[b, p]) + (0,) * (len(shape) - 2))

    kv_shape = (1, 1, FOX_HEADS, FOX_DH, page)
    lf_shape = (1, 1, FOX_HEADS, page)
    in_specs = [zspec(C_FQ, FOX_W), zspec(C_FK, FOX_W), zspec(C_FV, FOX_W), zspec(C_FF, LANES)]
    in_specs += [pspec(kv_shape, p) for p in range(n_pages)]
    in_specs += [pspec(kv_shape, p) for p in range(n_pages)]
    in_specs += [pspec(lf_shape, p) for p in range(n_pages)]
    grid_spec = pltpu.PrefetchScalarGridSpec(
        num_scalar_prefetch=1,
        grid=(nb,),
        in_specs=in_specs,
        out_specs=[pl.BlockSpec((seq, FOX_W), lambda b, pt: (b, 0)),
                   pl.BlockSpec((seq, LANES), lambda b, pt: (b, 0))],
    )
    return pl.pallas_call(
        functools.partial(_fox_decode_kernel, n_pages=n_pages, page=page, seq=seq),
        grid_spec=grid_spec,
        out_shape=[jax.ShapeDtypeStruct((n, FOX_W), F32),
                   jax.ShapeDtypeStruct((n, LANES), F32)],
        compiler_params=_cparams("arbitrary"),
        name="fox_decode",
    )(page_table, z, z, z, z, *([cache_kt] * n_pages), *([cache_vt] * n_pages), *([cache_lft] * n_pages))


def _mix_kernel(x_ref, ya_ref, yb_ref, ga_ref, gb_ref, woa_ref, wob_ref, wo_ref, o_ref):
    ya = _dot_f32(ya_ref[...].astype(BF16), woa_ref[...])
    yb = _dot_f32(yb_ref[...].astype(BF16), wob_ref[...])
    merged = _sigmoid(ga_ref[...]) * ya + _sigmoid(gb_ref[...]) * yb
    o_ref[...] = x_ref[...] + _dot_f32(merged.astype(BF16), wo_ref[...])


def _mix(x, ya, yb, z, woa, wob, wo):
    n, d = x.shape
    assert d == C_GB - C_GA
    tm = _pick_tile(n, 704)
    row = lambda w: pl.BlockSpec((tm, w), lambda i: (i, 0))
    full = lambda a: pl.BlockSpec(a.shape, lambda i: (0, 0))
    return pl.pallas_call(
        _mix_kernel,
        grid=(n // tm,),
        in_specs=[row(d), row(HG_W), row(FOX_W),
                  pl.BlockSpec((tm, d), lambda i: (i, C_GA // d)),
                  pl.BlockSpec((tm, d), lambda i: (i, C_GB // d)),
                  full(woa), full(wob), full(wo)],
        out_specs=row(d),
        out_shape=jax.ShapeDtypeStruct((n, d), F32),
        compiler_params=_cparams("parallel"),
        name="mix_out",
    )(x, ya, yb, z, z, woa, wob, wo)


def _ffn_kernel(x_ref, g_ref, wa_ref, wg_ref, cw_ref, cb_ref, wd_ref, ext_ref, o_ref, at_ref,
                h_scr, carry_scr, *, seq, tiles_per_seq):
    i, j = pl.program_id(0), pl.program_id(1)
    tm = x_ref.shape[0]

    @pl.when(j == 0)
    def _():
        x = x_ref[...]
        h_scr[...] = _rmsnorm(x, g_ref[...]).astype(BF16)
        o_ref[...] = x

    h = h_scr[...]
    a = _dot_f32(h, wa_ref[...])
    gate = _dot_f32(h, wg_ref[...])
    row = _iota(a.shape, 0)
    r1 = pltpu.roll(a, 1, 0)
    r2 = pltpu.roll(a, 2, 0)
    if tiles_per_seq:
        @pl.when((i == 0) & (j == 0))
        def _():
            carry_scr[...] = jnp.zeros_like(carry_scr)

        prev = carry_scr[j]
        seq_start = i % tiles_per_seq == 0
        p1 = jnp.where(seq_start, 0.0, prev[7:8])
        p2 = jnp.where(seq_start, 0.0, prev[6:7])
        a1 = jnp.where(row == 0, p1, r1)
        a2 = jnp.where(row == 0, p2, jnp.where(row == 1, p1, r2))
        carry_scr[j] = a[tm - 8:tm]
    else:
        ext = ext_ref[...]
        pos = row % seq
        a1 = jnp.where(pos == 0, pltpu.roll(ext, tm - 1, 0), r1)
        a2 = jnp.where(pos < 2, ext, r2)
    cw = cw_ref[...]
    conv = cb_ref[...] + cw[0:1] * a2 + cw[1:2] * a1 + cw[2:3] * a
    act = conv * _sigmoid(conv) * gate
    o_ref[...] += _dot_f32(act.astype(BF16), wd_ref[...])
    at_ref[...] = a[tm - at_ref.shape[0]:]


def _ffn(x, g, w_up, conv_w, conv_b, w_down, ext, seq):
    n, d = x.shape
    d_ff = w_down.shape[0]
    tf = 256
    assert d_ff % tf == 0
    nj = d_ff // tf
    if ext is None:
        tm = _pick_tile(seq, 1040)
        tiles_per_seq = seq // tm
        out_rows = 8
        ext = jnp.zeros((8, d_ff), F32)
        ext_spec = pl.BlockSpec((8, tf), lambda i, j: (0, j))
    else:
        tm = _pick_tile(n, 1040)
        assert tm % seq == 0 and seq == 8
        tiles_per_seq = 0
        out_rows = tm
        ext_spec = pl.BlockSpec((tm, tf), lambda i, j: (i, j))
    nt = n // tm
    out, a_tail = pl.pallas_call(
        functools.partial(_ffn_kernel, seq=seq, tiles_per_seq=tiles_per_seq),
        grid=(nt, nj),
        in_specs=[
            pl.BlockSpec((tm, d), lambda i, j: (i, 0)),
            pl.BlockSpec((1, d), lambda i, j: (0, 0)),
            pl.BlockSpec((d, tf), lambda i, j: (0, j)),
            pl.BlockSpec((d, tf), lambda i, j: (0, j + nj)),
            pl.BlockSpec((CONV_W, tf), lambda i, j: (0, j)),
            pl.BlockSpec((1, tf), lambda i, j: (0, j)),
            pl.BlockSpec((tf, d), lambda i, j: (j, 0)),
            ext_spec,
        ],
        out_specs=[pl.BlockSpec((tm, d), lambda i, j: (i, 0)),
                   pl.BlockSpec((out_rows, tf), lambda i, j: (i, j))],
        out_shape=[jax.ShapeDtypeStruct((n, d), F32),
                   jax.ShapeDtypeStruct((nt * out_rows, d_ff), F32)],
        scratch_shapes=[pltpu.VMEM((tm, d), BF16), pltpu.VMEM((nj, 8, tf), F32)],
        compiler_params=_cparams("arbitrary", "arbitrary"),
        name="conv_ffn",
    )(x, g.reshape(1, d), w_up, w_up, conv_w, conv_b.reshape(1, d_ff), w_down, ext)
    return out, a_tail, tiles_per_seq


def _norm_kernel(x_ref, g_ref, o_ref):
    o_ref[...] = _rmsnorm(x_ref[...], g_ref[...])


def _final_norm(x, g):
    n, d = x.shape
    tm = _pick_tile(n, 1040)
    return pl.pallas_call(
        _norm_kernel,
        grid=(n // tm,),
        in_specs=[pl.BlockSpec((tm, d), lambda i: (i, 0)), pl.BlockSpec((1, d), lambda i: (0, 0))],
        out_specs=pl.BlockSpec((tm, d), lambda i: (i, 0)),
        out_shape=jax.ShapeDtypeStruct((n, d), F32),
        compiler_params=_cparams("parallel"),
        name="final_norm",
    )(x, g.reshape(1, d))


def _norm_tail_kernel(x_ref, g_ref, o_ref):
    skip = x_ref.shape[1] - o_ref.shape[1]
    o_ref[0] = _rmsnorm(x_ref[0, skip:, :], g_ref[...])


def _final_norm_tail(x3, g, keep):
    nb, seq, d = x3.shape
    assert (seq - keep) % 8 == 0
    return pl.pallas_call(
        _norm_tail_kernel,
        grid=(nb,),
        in_specs=[pl.BlockSpec((1, seq, d), lambda b: (b, 0, 0)), pl.BlockSpec((1, d), lambda b: (0, 0))],
        out_specs=pl.BlockSpec((1, keep, d), lambda b: (b, 0, 0)),
        out_shape=jax.ShapeDtypeStruct((nb, keep, d), F32),
        compiler_params=_cparams("parallel"),
        name="final_norm_tail",
    )(x3, g.reshape(1, d))


def _reorder_in_proj(w_in, b_in):
    n_hf = 4 * HG_W + 3 * FOX_W
    pad = LANES - FOX_HEADS

    def reorder(a):
        parts = [a[..., n_hf + FOX_HEADS:], a[..., :n_hf], a[..., n_hf:n_hf + FOX_HEADS],
                 jnp.zeros(a.shape[:-1] + (pad,), a.dtype)]
        return jnp.concatenate(parts, axis=-1)

    return reorder(w_in).astype(BF16), reorder(b_in)


def kernel(x_prompt, x_sample, cache_k, cache_v, cache_logf, state_hgrn, state_conv, page_table, meta_tokens, norm1, norm2, norm_f, w_in, b_in, hg_lower_bounds, hg_norm, w_oa, w_ob, w_o, w_up, conv_w, conv_b, w_down):
    nb, s_len, d = x_prompt.shape
    seq_p = s_len + N_META
    db, seq_s, _ = x_sample.shape
    depth = w_in.shape[0]
    d_ff = w_down.shape[1]
    n_pool, page = cache_k.shape[1], cache_k.shape[2]
    assert w_in.shape[2] == 4 * HG_W + 3 * FOX_W + FOX_HEADS + 2 * d and 2 * d == C_ZQ

    w_in_r, b_in_r = _reorder_in_proj(w_in, b_in)
    w_oa, w_ob, w_o, w_up, w_down = (w.astype(BF16) for w in (w_oa, w_ob, w_o, w_up, w_down))
    cache_kt = jnp.transpose(cache_k, (0, 1, 3, 4, 2))
    cache_vt = jnp.transpose(cache_v, (0, 1, 3, 4, 2))
    cache_lft = jnp.swapaxes(cache_logf, 2, 3)
    lbraw = hg_lower_bounds.astype(F32)

    meta = jnp.broadcast_to(meta_tokens.astype(x_prompt.dtype)[None], (nb, N_META, d))
    xp = jnp.concatenate([meta, x_prompt], axis=1).reshape(nb * seq_p, d)
    xs = x_sample.reshape(db * seq_s, d)

    outs = [[] for _ in range(5)]
    kv_p = kv_s = hg_s = None
    for l in range(depth):
        z = _inproj(xp, norm1[l], w_in_r[l], b_in_r[l])
        z3 = z.reshape(nb, seq_p, NZ)
        *kv_p, lf, c = _fox_prep(z3, kv_p, l, depth)
        y_hg, s_new = _hgrn_prompt(z3, lbraw, hg_norm[l], l)
        o_fox = _fox_attn(z3, c)
        xp = _mix(xp, y_hg.reshape(nb * seq_p, HG_W), o_fox.reshape(nb * seq_p, FOX_W), z, w_oa[l], w_ob[l], w_o[l])
        xp, a_tail, tps = _ffn(xp, norm2[l], w_up[l], conv_w[l], conv_b[l], w_down[l], None, seq_p)
        outs[0].append(lf[:, :, :FOX_HEADS])
        outs[1].append(s_new)
        outs[2].append(a_tail.reshape(nb, tps, 8, d_ff)[:, tps - 1, 8 - (CONV_W - 1):])
        z, *kv_s = _inproj(xs, norm1[l], w_in_r[l], b_in_r[l], depth, kv_s, l)
        y_hg, hg_s = _hgrn_sample(z, state_hgrn, lbraw, hg_norm[l], l, seq_s, hg_s)
        o_fox, lf = _fox_decode(z, cache_kt, cache_vt, cache_lft, page_table, l, seq_s)
        xs = _mix(xs, y_hg, o_fox, z, w_oa[l], w_ob[l], w_o[l])
        ext = jnp.pad(state_conv[l], ((0, 0), (0, seq_s - (CONV_W - 1)), (0, 0))).reshape(db * seq_s, d_ff)
        xs, a_all, _ = _ffn(xs, norm2[l], w_up[l], conv_w[l], conv_b[l], w_down[l], ext, seq_s)
        outs[3].append(lf.reshape(db, seq_s, LANES)[:, :, :FOX_HEADS])
        outs[4].append(a_all.reshape(db, seq_s, d_ff)[:, seq_s - (CONV_W - 1):])
    y_prompt = _final_norm_tail(xp.reshape(nb, seq_p, d), norm_f, s_len)
    y_sample = _final_norm(xs, norm_f).reshape(db, seq_s, d)
    lf_p, hg_p, cv_p, lf_s, cv_s = (jnp.stack(o) for o in outs)
    k_p, v_p = (jnp.transpose(a.reshape(depth, nb, FOX_HEADS, FOX_DH, seq_p), (0, 1, 4, 2, 3)) for a in kv_p)
    k_s, v_s = (a.reshape(depth, db, seq_s, FOX_HEADS, FOX_DH) for a in kv_s)
    return (y_prompt, y_sample, k_p, v_p, lf_p, hg_p, cv_p, k_s, v_s, lf_s, hg_s, cv_s)
```

```python
import functools

import jax
import jax.numpy as jnp
from jax import lax
from jax.experimental import pallas as pl
from jax.experimental.pallas import tpu as pltpu

F32 = jnp.float32
BF16 = jnp.bfloat16
EPS = 1e-6
NEG_INF = float("-inf")
LOG2E = 1.4426950408889634

N_META = 16
HG_HEADS = 4
HG_DK = 128
FOX_HEADS = 8
FOX_DH = 64
CONV_W = 3

LANES = 128
HG_EXACT = 8
HG_GROUP = 4
FOX_PRUNE = 140.0
VMEM_LIMIT = 56 * 1024 * 1024

HG_W = HG_HEADS * HG_DK
FOX_W = FOX_HEADS * FOX_DH
C_GA = 0
C_GB = 1024
C_ZQ = 2048
C_ZF = C_ZQ + HG_W
C_ZI = C_ZF + HG_W
C_ZG = C_ZI + HG_W
C_FQ = C_ZG + HG_W
C_FK = C_FQ + FOX_W
C_FV = C_FK + FOX_W
C_FF = C_FV + FOX_W
NZ = C_FF + LANES


def _cparams(*sem):
    return pltpu.CompilerParams(dimension_semantics=sem, vmem_limit_bytes=VMEM_LIMIT)


def _pick_tile(n, cap):
    best = None
    for d in range(8, min(n, cap) + 1, 8):
        if n % d == 0:
            best = d
    assert best is not None, (n, cap)
    return best


def _sigmoid(x):
    return 0.5 * jnp.tanh(0.5 * x) + 0.5


def _log_sigmoid(x):
    return jnp.minimum(x, 0.0) - jnp.log1p(jnp.exp(-jnp.abs(x)))


def _split3(x):
    hi = x.astype(BF16)
    r = x - hi.astype(F32)
    mid = r.astype(BF16)
    r = r - mid.astype(F32)
    return hi, mid, r.astype(BF16)


def _dot_f32(a, b, dims=None):
    if dims is None:
        return jnp.dot(a, b, preferred_element_type=F32)
    return lax.dot_general(a, b, (dims, ((), ())), preferred_element_type=F32)


NT = ((1,), (1,))


def _mask_dot_left(m, x):
    return sum(_dot_f32(m, p) for p in _split3(x))


def _mask_dot_right(x, m):
    return sum(_dot_f32(p, m) for p in _split3(x))


def _rmsnorm(x, g):
    return x * lax.rsqrt(jnp.mean(x * x, axis=-1, keepdims=True) + EPS) * g


def _iota(shape, dim):
    return lax.broadcasted_iota(jnp.int32, shape, dim)


def _inproj_kernel(x_ref, g_ref, w_ref, b_ref, *rest, tn, with_kv):
    if with_kv:
        z_ref, k_ref, v_ref, h_scr = rest[-4:]
    else:
        z_ref, h_scr = rest
    j = pl.program_id(1)

    @pl.when(j == 0)
    def _():
        h_scr[...] = _rmsnorm(x_ref[...], g_ref[...]).astype(BF16)

    z = _dot_f32(h_scr[...], w_ref[...]) + b_ref[...]
    z_ref[...] = z

    if with_kv:
        @pl.when(j == C_FK // tn)
        def _():
            k_ref[...] = z[:, C_FK % tn:C_FK % tn + FOX_W]
            v_ref[...] = z[:, C_FV % tn:C_FV % tn + FOX_W]


def _inproj(x, g, w, b, kv_depth=0, kv=None, layer=0):
    n, d = x.shape
    tm = _pick_tile(n, 1040)
    tn = 1152
    assert NZ % tn == 0 and C_FK // tn == (C_FV + FOX_W - 1) // tn
    in_specs = [
        pl.BlockSpec((tm, d), lambda i, j: (i, 0)),
        pl.BlockSpec((1, d), lambda i, j: (0, 0)),
        pl.BlockSpec((d, tn), lambda i, j: (0, j)),
        pl.BlockSpec((1, tn), lambda i, j: (0, j)),
    ]
    out_specs = [pl.BlockSpec((tm, tn), lambda i, j: (i, j))]
    out_shape = [jax.ShapeDtypeStruct((n, NZ), F32)]
    args = [x, g.reshape(1, d), w, b.reshape(1, NZ)]
    aliases = {}
    if kv_depth:
        out_specs += [pl.BlockSpec((None, tm, FOX_W), lambda i, j: (layer, i, 0))] * 2
        out_shape += [jax.ShapeDtypeStruct((kv_depth, n, FOX_W), F32)] * 2
    if kv is not None:
        in_specs += [pl.BlockSpec(memory_space=pl.ANY)] * 2
        args += list(kv)
        aliases = {4: 1, 5: 2}
    out = pl.pallas_call(
        functools.partial(_inproj_kernel, tn=tn, with_kv=bool(kv_depth)),
        grid=(n // tm, NZ // tn),
        in_specs=in_specs,
        out_specs=out_specs,
        out_shape=out_shape,
        input_output_aliases=aliases,
        scratch_shapes=[pltpu.VMEM((tm, d), BF16)],
        compiler_params=_cparams("parallel", "arbitrary"),
        name="inproj",
    )(*args)
    return out if kv_depth else out[0]


def _hgrn_lower_bound(lbraw, layer):
    e = jnp.exp(lbraw - jnp.max(lbraw, axis=0, keepdims=True))
    sm = e / jnp.sum(e, axis=0, keepdims=True)
    lb = jnp.zeros_like(sm[0:1])
    for i in range(1, layer + 1):
        lb = lb + sm[i:i + 1]
    return lb


def _hgrn_gates(zq, zf, zi, lb):
    q = zq * _sigmoid(zq)
    a = jnp.log(lb)
    b = jnp.log1p(-lb) + _log_sigmoid(zf)
    logf = jnp.maximum(a, b) + jnp.log1p(jnp.exp(-jnp.abs(a - b)))
    k = (1.0 - lb) * _sigmoid(-zf)
    return q, k, zi, logf


def _hgrn_diag(q, k, b, v):
    n = q.shape[0]
    rowid = _iota(q.shape, 0)
    acc = jnp.zeros_like(v)
    for s in range(n):
        d = jnp.where(rowid >= s, b - b[s:s + 1], NEG_INF)
        w = q * k[s:s + 1] * jnp.exp2(d)
        acc = acc + jnp.sum(w, axis=-1, keepdims=True) * v[s:s + 1]
    return acc


def _hgrn_head_out(o, zg, gain):
    o = o * lax.rsqrt(jnp.mean(o * o, axis=-1, keepdims=True) + EPS)
    return o * gain * (zg * _sigmoid(zg))


def _pad_rows(x, rows):
    if x.shape[0] == rows:
        return x
    return jnp.concatenate([x, jnp.zeros((rows - x.shape[0], x.shape[1]), x.dtype)], axis=0)


def _hgrn_prompt_kernel(zq_ref, zf_ref, zi_ref, zg_ref, lbraw_ref, gain_ref, y_ref, s_ref, st_scr,
                        *, layer, seq):
    ck = LANES
    n_full, tail = divmod(seq, ck)
    assert tail % HG_EXACT == 0
    lb_all = _hgrn_lower_bound(lbraw_ref[...], layer)
    gain_all = gain_ref[...]
    row = _iota((ck, ck), 0)
    col = _iota((ck, ck), 1)
    ltri = jnp.where(col <= row, 1.0, 0.0).astype(BF16)
    levels = []
    half = HG_EXACT
    while half < ck:
        levels.append((half, row % (2 * half) >= half, row // (2 * half) == col // (2 * half)))
        half *= 2
    st_scr[...] = jnp.zeros_like(st_scr)

    def chunk(r0, n_rows):
        for g in range(HG_GROUP):
            head_chunk(r0, n_rows, g, slice(HG_DK * g, HG_DK * (g + 1)))

    def head_chunk(r0, n_rows, g, hs):
        sl = pl.ds(r0, n_rows)
        lb, gain = lb_all[:, hs], gain_all[:, hs]
        q, k, v, logf = _hgrn_gates(zq_ref[0, sl, hs], zf_ref[0, sl, hs], zi_ref[0, sl, hs], lb)
        q, k, v, logf = (_pad_rows(a, ck) for a in (q, k, v, logf))
        b = _mask_dot_left(ltri, logf * LOG2E)
        bend = b[ck - 1:ck]
        a = jnp.zeros((ck, ck), F32)
        for half, upper, same_blk in levels:
            blk = 2 * half
            bmid = jnp.concatenate(
                [jnp.broadcast_to(b[blk * i + half - 1:blk * i + half], (blk, ck)) for i in range(ck // blk)],
                axis=0)
            q2 = jnp.where(upper, q * jnp.exp2(jnp.minimum(b - bmid, 0.0)), 0.0).astype(BF16)
            k2 = jnp.where(upper, 0.0, k * jnp.exp2(jnp.minimum(bmid - b, 0.0))).astype(BF16)
            a = a + jnp.where(same_blk, _dot_f32(q2, k2, NT), 0.0)
        st = st_scr[g]
        o = _dot_f32(a.astype(BF16), v.astype(BF16))
        o = o + _dot_f32((q * jnp.exp2(b)).astype(BF16), st.astype(BF16), NT)
        diag = [_hgrn_diag(*(x[HG_EXACT * i:HG_EXACT * (i + 1)] for x in (q, k, b, v)))
                for i in range(n_rows // HG_EXACT)]
        o = o[:n_rows] + jnp.concatenate(diag, axis=0)
        st_scr[g] = st * jnp.exp2(bend) + _dot_f32(v.T.astype(BF16), (k * jnp.exp2(bend - b)).astype(BF16))
        y_ref[0, sl, hs] = _hgrn_head_out(o, zg_ref[0, sl, hs], gain).astype(y_ref.dtype)

    def body(c, carry):
        chunk(pl.multiple_of(c * ck, ck), ck)
        return carry

    lax.fori_loop(0, n_full, body, 0)
    if tail:
        chunk(n_full * ck, tail)
    for g in range(HG_GROUP):
        s_ref[0, g] = st_scr[g].T


def _hgrn_prompt(z3, lbraw, gain, layer):
    nb, seq, _ = z3.shape
    gw = HG_GROUP * HG_DK

    def zspec(c0):
        return pl.BlockSpec((1, seq, gw), lambda b, h, c0=c0: (b, 0, c0 // gw + h))

    return pl.pallas_call(
        functools.partial(_hgrn_prompt_kernel, layer=layer, seq=seq),
        grid=(nb, HG_HEADS // HG_GROUP),
        in_specs=[zspec(C_ZQ), zspec(C_ZF), zspec(C_ZI), zspec(C_ZG),
                  pl.BlockSpec((lbraw.shape[0], gw), lambda b, h: (0, h)),
                  pl.BlockSpec((1, gw), lambda b, h: (0, h))],
        out_specs=[pl.BlockSpec((1, seq, gw), lambda b, h: (b, 0, h)),
                   pl.BlockSpec((1, HG_GROUP, HG_DK, HG_DK), lambda b, h: (b, h, 0, 0))],
        out_shape=[jax.ShapeDtypeStruct((nb, seq, HG_W), BF16),
                   jax.ShapeDtypeStruct((nb, HG_HEADS, HG_DK, HG_DK), F32)],
        scratch_shapes=[pltpu.VMEM((HG_GROUP, HG_DK, HG_DK), F32)],
        compiler_params=_cparams("parallel", "parallel"),
        name="hgrn_prompt",
    )(z3, z3, z3, z3, lbraw, gain.reshape(1, HG_W))


def _hgrn_sample_kernel(zq_ref, zf_ref, zi_ref, zg_ref, lbraw_ref, gain_ref, s0_ref, *rest, layer, seq):
    y_ref, s_ref = rest[-2:]
    rows = zq_ref.shape[0]
    lb_all = _hgrn_lower_bound(lbraw_ref[...], layer)
    row = _iota((rows, rows), 0)
    col = _iota((rows, rows), 1)
    ltri = jnp.where((row // seq == col // seq) & (col <= row), 1.0, 0.0).astype(BF16)
    for g in range(HG_HEADS):
        hs = slice(HG_DK * g, HG_DK * (g + 1))
        _hgrn_sample_head(zq_ref, zf_ref, zi_ref, zg_ref, gain_ref, s0_ref, y_ref, s_ref, lb_all[:, hs], ltri,
                          g, hs, seq)


def _hgrn_sample_head(zq_ref, zf_ref, zi_ref, zg_ref, gain_ref, s0_ref, y_ref, s_ref, lb, ltri, g, hs, seq):
    rows = zq_ref.shape[0]
    n_seq = rows // seq
    q, k, v, logf = _hgrn_gates(zq_ref[:, hs], zf_ref[:, hs], zi_ref[:, hs], lb)
    b = _mask_dot_left(ltri, logf * LOG2E)
    bend = jnp.concatenate(
        [jnp.broadcast_to(b[seq * s + seq - 1:seq * (s + 1)], (seq, LANES)) for s in range(n_seq)], axis=0)
    qt = (q * jnp.exp2(b)).astype(BF16)
    kt = k * jnp.exp2(bend - b)
    outs = []
    for s in range(n_seq):
        rs = slice(seq * s, seq * (s + 1))
        s0 = s0_ref[s, g]
        outs.append(_dot_f32(qt[rs], s0.astype(BF16)) + _hgrn_diag(q[rs], k[rs], b[rs], v[rs]))
        x = jnp.concatenate([kt[rs], jnp.exp2(bend[rs]), jnp.zeros((HG_DK - 2 * seq, LANES), F32)], axis=0)
        xt = x.T
        inc = _dot_f32(xt.astype(BF16), _pad_rows(v[rs], HG_DK).astype(BF16))
        s_ref[s, g] = xt[:, seq:seq + 1] * s0 + inc
    o = jnp.concatenate(outs, axis=0)
    y_ref[:, hs] = _hgrn_head_out(o, zg_ref[:, hs], gain_ref[:, hs]).astype(y_ref.dtype)


def _hgrn_sample(z, state, lbraw, gain, layer, seq, new_state):
    n = z.shape[0]
    rows = LANES
    assert n % rows == 0 and rows % seq == 0 and 2 * seq <= HG_DK
    sblk = rows // seq

    def zspec(c0):
        return pl.BlockSpec((rows, HG_W), lambda i, c0=c0: (i, c0 // HG_W))

    st_spec = pl.BlockSpec((None, sblk, HG_HEADS, HG_DK, HG_DK), lambda i: (layer, i, 0, 0, 0))
    in_specs = [zspec(C_ZQ), zspec(C_ZF), zspec(C_ZI), zspec(C_ZG),
                pl.BlockSpec(lbraw.shape, lambda i: (0, 0)),
                pl.BlockSpec((1, HG_W), lambda i: (0, 0)),
                st_spec]
    args = [z, z, z, z, lbraw, gain.reshape(1, HG_W), state]
    aliases = {}
    if new_state is not None:
        in_specs.append(pl.BlockSpec(memory_space=pl.ANY))
        args.append(new_state)
        aliases = {7: 1}
    return pl.pallas_call(
        functools.partial(_hgrn_sample_kernel, layer=layer, seq=seq),
        grid=(n // rows,),
        in_specs=in_specs,
        out_specs=[pl.BlockSpec((rows, HG_W), lambda i: (i, 0)), st_spec],
        out_shape=[jax.ShapeDtypeStruct((n, HG_W), BF16),
                   jax.ShapeDtypeStruct(state.shape, F32)],
        input_output_aliases=aliases,
        compiler_params=_cparams("parallel"),
        name="hgrn_sample",
    )(*args)


def _fox_gate(ff_ref, lf_ref, c_ref, seq):
    ck = LANES
    n_full, tail = divmod(seq, ck)
    ltri = jnp.where(_iota((ck, ck), 1) <= _iota((ck, ck), 0), 1.0, 0.0).astype(BF16)

    def chunk(r0, n_rows, carry):
        sl = pl.ds(r0, n_rows)
        lf = _log_sigmoid(ff_ref[0, sl, :])
        lf_ref[0, sl, :] = lf
        c = _mask_dot_left(ltri, _pad_rows(lf, ck)) + carry
        c_ref[0, sl, :] = c[:n_rows]
        return c[n_rows - 1:n_rows]

    carry = lax.fori_loop(0, n_full, lambda i, cr: chunk(pl.multiple_of(i * ck, ck), ck, cr),
                          jnp.zeros((1, LANES), F32))
    if tail:
        chunk(n_full * ck, tail, carry)


def _fox_attn_kernel(fq_ref, fk_ref, fv_ref, c_ref, o_ref, qa_scr, ka_scr, va_scr, r_scr, acc_scr,
                     qn_scr, kn_scr, *, seq, bq):
    padded = qa_scr.shape[1]
    nq = padded // bq
    ck = LANES
    n_full, tail = divmod(seq, ck)
    scale = FOX_DH ** -0.5 * LOG2E

    lane8 = _iota((8, LANES), 1)

    def max_sq_norm(x, own):
        return jnp.max(jnp.sum(jnp.where(own, x * x, 0.0), axis=-1, keepdims=True), axis=0, keepdims=True)

    def build(i, r0, n_rows):
        sl = pl.ds(r0, n_rows)
        cb = c_ref[0, sl, :]
        lane = _iota((n_rows, LANES), 1)
        qn = jnp.zeros((8, LANES), F32)
        kn = jnp.zeros((8, LANES), F32)
        for h in range(FOX_HEADS):
            hh = h % 2
            ps = slice(LANES * (h // 2), LANES * (h // 2 + 1))
            qf, kf, vf = fq_ref[0, sl, ps], fk_ref[0, sl, ps], fv_ref[0, sl, ps]
            ch = jnp.sum(jnp.where(lane == h, cb, 0.0), axis=-1, keepdims=True) * LOG2E
            hi, mid, lo = (p.astype(F32) for p in _split3(ch))
            own = (lane >= FOX_DH * hh) & (lane < FOX_DH * (hh + 1))
            f0 = FOX_DH * (1 - hh)
            ones_q = (lane >= f0 + 3) & (lane < f0 + 6)
            ones_k = (lane >= f0) & (lane < f0 + 3)
            qa = jnp.where(own, qf * scale,
                           jnp.where(lane == f0, hi, jnp.where(lane == f0 + 1, mid, jnp.where(
                               lane == f0 + 2, lo, jnp.where(ones_q, 1.0, 0.0)))))
            ka = jnp.where(own, kf,
                           jnp.where(ones_k, 1.0, jnp.where(lane == f0 + 3, -hi, jnp.where(
                               lane == f0 + 4, -mid, jnp.where(lane == f0 + 5, -lo, 0.0)))))
            qa_scr[h, sl, :] = qa.astype(BF16)
            ka_scr[h, sl, :] = ka.astype(BF16)
            va_scr[h, sl, :] = jnp.where(own, vf, 1.0).astype(BF16)
            qn = jnp.where(lane8 == h, max_sq_norm(qf, own), qn)
            kn = jnp.where(lane8 == h, max_sq_norm(kf, own), kn)
        qn_scr[i] = qn
        kn_scr[i] = kn

    def build_body(i, carry):
        build(i, pl.multiple_of(i * ck, ck), ck)
        return carry

    lax.fori_loop(0, n_full, build_body, 0)
    if tail:
        build(n_full, n_full * ck, tail)
    kmax = jnp.max(kn_scr[...], axis=0)
    if padded > seq:
        zpad = jnp.zeros((padded - seq, LANES), BF16)
        for h in range(FOX_HEADS):
            qa_scr[h, seq:padded, :] = zpad
            ka_scr[h, seq:padded, :] = zpad
            va_scr[h, seq:padded, :] = zpad

    kc = 2 * bq

    def qblock(q0, n_chunks, tail_k0, tail_w, rows):
        qa = [qa_scr[h, pl.ds(q0, rows), :] for h in range(FOX_HEADS)]
        tail_mask = _iota((rows, tail_w), 1) <= _iota((rows, tail_w), 0) + (tail_w - bq)
        lane_q = _iota((rows, LANES), 1)

        def scores(hh, k0, width, mask):
            s = _dot_f32(qa[hh], ka_scr[hh, pl.ds(k0, width), :], NT)
            return s if mask is None else jnp.where(mask, s, NEG_INF)

        def step(k0, width, mask):
            for hh in range(FOX_HEADS):
                s = scores(hh, k0, width, mask)
                m_old = r_scr[hh, :rows]
                blk = s[:, :LANES]
                for t in range(1, width // LANES):
                    blk = jnp.maximum(blk, s[:, LANES * t:LANES * (t + 1)])
                m_new = jnp.maximum(m_old, jnp.broadcast_to(jnp.max(blk, axis=-1, keepdims=True), (rows, LANES)))
                p = jnp.exp2(s - jnp.concatenate([m_new] * (width // LANES), axis=1))
                acc_scr[hh, :rows] = jnp.exp2(m_old - m_new) * acc_scr[hh, :rows] + _dot_f32(
                    p.astype(BF16), va_scr[hh, pl.ds(k0, width), :])
                r_scr[hh, :rows] = m_new

        r_scr[...] = jnp.full(r_scr.shape, NEG_INF, F32)
        acc_scr[...] = jnp.zeros_like(acc_scr)
        step(tail_k0, tail_w, tail_mask)
        i0 = q0 // ck
        qmax = qn_scr[i0]
        for t in range(1, -(-rows // ck)):
            qmax = jnp.maximum(qmax, qn_scr[i0 + t])
        mmin = jnp.zeros((8, LANES), F32)
        for h in range(FOX_HEADS):
            mmin = jnp.where(lane8 == h, jnp.min(r_scr[h, :rows], axis=0, keepdims=True), mmin)
        thr = (mmin - FOX_PRUNE - jnp.sqrt(qmax * kmax) * (scale * 1.05))[0:1]
        c_q = c_ref[0, pl.ds(q0, 1), :]

        def body(jj, carry):
            k0 = pl.multiple_of((n_chunks - 1 - jj) * kc, kc)
            c_k = c_ref[0, pl.ds(k0 + kc - 1, 1), :]
            reach = jnp.where(lane8[0:1] < FOX_HEADS, (c_q - c_k) * LOG2E - thr, NEG_INF)

            @pl.when(jnp.max(reach) >= 0.0)
            def _():
                step(k0, kc, None)
            return carry

        lax.fori_loop(0, n_chunks, body, 0)
        outs = [acc_scr[h, :rows] / pltpu.roll(acc_scr[h, :rows], FOX_DH, 1) for h in range(FOX_HEADS)]
        for pr in range(FOX_HEADS // 2):
            o = jnp.where(lane_q < FOX_DH, outs[2 * pr], outs[2 * pr + 1])
            o_ref[0, pl.ds(q0, rows), LANES * pr:LANES * (pr + 1)] = o.astype(o_ref.dtype)

    def even_block(u, rows=bq):
        q0 = pl.multiple_of(u * kc, kc) if not isinstance(u, int) else u * kc
        qblock(q0, u, q0, bq, rows)

    def odd_block(u, rows=bq):
        k0 = pl.multiple_of(u * kc, kc) if not isinstance(u, int) else u * kc
        qblock(k0 + bq, u, k0, kc, rows)

    def pair_body(u, carry):
        even_block(u)
        odd_block(u)
        return carry

    n_whole = seq // bq
    lax.fori_loop(0, n_whole // 2, pair_body, 0)
    rest = [(i, bq) for i in range(n_whole - n_whole % 2, n_whole)]
    if n_whole < nq:
        rest.append((n_whole, seq - n_whole * bq))
    for i, rows in rest:
        (odd_block if i % 2 else even_block)(i // 2, rows)


def _fox_attn(z3, c):
    nb, seq, _ = z3.shape
    bq = 256
    padded = -(-seq // bq) * bq

    def zspec(c0):
        return pl.BlockSpec((1, seq, FOX_W), lambda b, c0=c0: (b, 0, c0 // FOX_W))

    return pl.pallas_call(
        functools.partial(_fox_attn_kernel, seq=seq, bq=bq),
        grid=(nb,),
        in_specs=[zspec(C_FQ), zspec(C_FK), zspec(C_FV),
                  pl.BlockSpec((1, seq, LANES), lambda b: (b, 0, 0))],
        out_specs=pl.BlockSpec((1, seq, FOX_W), lambda b: (b, 0, 0)),
        out_shape=jax.ShapeDtypeStruct((nb, seq, FOX_W), BF16),
        scratch_shapes=[pltpu.VMEM((FOX_HEADS, padded, LANES), BF16)] * 3
        + [pltpu.VMEM((FOX_HEADS, bq, LANES), F32)] * 2
        + [pltpu.VMEM((-(-seq // LANES), 8, LANES), F32)] * 2,
        compiler_params=_cparams("parallel"),
        name="fox_attn",
    )(z3, z3, z3, c)


def _fox_prep_kernel(fk_ref, fv_ref, ff_ref, *rest, seq):
    kt_ref, vt_ref, lf_ref, c_ref, tail_scr = rest[-5:]
    _fox_gate(ff_ref, lf_ref, c_ref, seq)
    n_full, tail = divmod(seq, LANES)
    for src, dst in ((fk_ref, kt_ref), (fv_ref, vt_ref)):
        for i in range(n_full):
            dst[:, LANES * i:LANES * (i + 1)] = src[0, LANES * i:LANES * (i + 1), :].T
        if tail:
            tail_scr[...] = _pad_rows(src[0, n_full * LANES:seq, :], LANES).T
            dst[:, n_full * LANES:seq] = tail_scr[:, :tail]


def _fox_prep(z3, stacks, layer, depth):
    nb, seq, _ = z3.shape
    kv_spec = pl.BlockSpec((None, None, FOX_W, seq), lambda b: (layer, b, 0, 0))
    g_spec = pl.BlockSpec((1, seq, LANES), lambda b: (b, 0, 0))
    in_specs = [pl.BlockSpec((1, seq, FOX_W), lambda b: (b, 0, C_FK // FOX_W)),
                pl.BlockSpec((1, seq, FOX_W), lambda b: (b, 0, C_FV // FOX_W)),
                pl.BlockSpec((1, seq, LANES), lambda b: (b, 0, C_FF // LANES))]
    args = [z3, z3, z3]
    aliases = {}
    if stacks is not None:
        in_specs += [pl.BlockSpec(memory_space=pl.ANY)] * 2
        args += list(stacks)
        aliases = {3: 0, 4: 1}
    kv_shape = jax.ShapeDtypeStruct((depth, nb, FOX_W, seq), F32)
    g_shape = jax.ShapeDtypeStruct((nb, seq, LANES), F32)
    return pl.pallas_call(
        functools.partial(_fox_prep_kernel, seq=seq),
        grid=(nb,),
        in_specs=in_specs,
        out_specs=[kv_spec, kv_spec, g_spec, g_spec],
        out_shape=[kv_shape, kv_shape, g_shape, g_shape],
        input_output_aliases=aliases,
        scratch_shapes=[pltpu.VMEM((FOX_W, LANES), F32)],
        compiler_params=_cparams("parallel"),
        name="fox_prep",
    )(*args)


def _fox_decode_kernel(pt_ref, fq_ref, fk_ref, fv_ref, ff_ref, *rest, n_pages, page, seq):
    del pt_ref
    k_refs, v_refs, lf_refs = rest[:n_pages], rest[n_pages:2 * n_pages], rest[2 * n_pages:3 * n_pages]
    o_ref, lfo_ref = rest[3 * n_pages:]
    assert page == LANES
    rows = FOX_HEADS * seq
    lfn = _log_sigmoid(ff_ref[...])
    lfo_ref[...] = lfn
    lfn = jnp.where(_iota(lfn.shape, 1) < FOX_HEADS, lfn, 0.0)
    lfn_t = _pad_rows(lfn, LANES).T[:FOX_HEADS]
    stack = jnp.concatenate([r[0, 0] for r in lf_refs] + [lfn_t], axis=0)
    ustrict = jnp.where(_iota((LANES, LANES), 0) > _iota((LANES, LANES), 1), 1.0, 0.0).astype(BF16)
    within = _mask_dot_right(stack, ustrict)
    tot = jnp.sum(stack, axis=-1, keepdims=True)
    g = [None] * (n_pages + 1)
    run = jnp.zeros((FOX_HEADS, 1), F32)
    for p in reversed(range(n_pages + 1)):
        rs = slice(FOX_HEADS * p, FOX_HEADS * (p + 1))
        g[p] = within[rs] + run
        run = run + tot[rs]
    gq = _pad_rows(g[n_pages], LANES).T[:seq]
    lane = _iota((seq, LANES), 1)
    gq_col = jnp.concatenate(
        [jnp.sum(jnp.where(lane == h, gq, 0.0), axis=-1, keepdims=True) for h in range(FOX_HEADS)], axis=0)

    def bias(gp):
        return jnp.concatenate(
            [jnp.broadcast_to(gp[h:h + 1], (seq, page)) for h in range(FOX_HEADS)], axis=0) - gq_col

    own = _iota((rows, FOX_W), 0) // seq == _iota((rows, FOX_W), 1) // FOX_DH
    q = fq_ref[...] * (FOX_DH ** -0.5)
    q_bd = jnp.where(own, jnp.concatenate([q] * FOX_HEADS, axis=0), 0.0).astype(BF16)
    s_blocks = []
    for p in range(n_pages):
        kt = k_refs[p][0, 0].reshape(FOX_W, page).astype(BF16)
        s_blocks.append(_dot_f32(q_bd, kt) + bias(g[p]))
    s = _dot_f32(q_bd, _pad_rows(fk_ref[...], page).astype(BF16), NT) + bias(g[n_pages])
    s_blocks.append(jnp.where(_iota((rows, page), 1) <= _iota((rows, page), 0) % seq, s, NEG_INF))
    mx = s_blocks[0]
    for s in s_blocks[1:]:
        mx = jnp.maximum(mx, s)
    m = jnp.max(mx, axis=-1, keepdims=True)
    wsum = jnp.zeros((rows, page), F32)
    acc = jnp.zeros((rows, FOX_W), F32)
    for p in range(n_pages + 1):
        pe = jnp.exp(s_blocks[p] - m)
        wsum = wsum + pe
        if p < n_pages:
            acc = acc + _dot_f32(pe.astype(BF16), v_refs[p][0, 0].reshape(FOX_W, page).astype(BF16), NT)
        else:
            acc = acc + _dot_f32(pe.astype(BF16), _pad_rows(fv_ref[...], page).astype(BF16))
    acc = jnp.where(own, acc / jnp.sum(wsum, axis=-1, keepdims=True), 0.0)
    o = acc[:seq]
    for h in range(1, FOX_HEADS):
        o = o + acc[seq * h:seq * (h + 1)]
    o_ref[...] = o


def _fox_decode(z, cache_kt, cache_vt, cache_lft, page_table, layer, seq):
    n = z.shape[0]
    nb, n_pages = page_table.shape
    page = cache_lft.shape[-1]
    assert nb * seq == n and seq == 8

    def zspec(c0, w):
        return pl.BlockSpec((seq, w), lambda b, pt, c0=c0, w=w: (b, c0 // w))

    def pspec(shape, p):
        return pl.BlockSpec(shape, lambda b, pt, p=p: (layer, pt[b, p]) + (0,) * (len(shape) - 2))

    kv_shape = (1, 1, FOX_HEADS, FOX_DH, page)
    lf_shape = (1, 1, FOX_HEADS, page)
    in_specs = [zspec(C_FQ, FOX_W), zspec(C_FK, FOX_W), zspec(C_FV, FOX_W), zspec(C_FF, LANES)]
    in_specs += [pspec(kv_shape, p) for p in range(n_pages)]
    in_specs += [pspec(kv_shape, p) for p in range(n_pages)]
    in_specs += [pspec(lf_shape, p) for p in range(n_pages)]
    grid_spec = pltpu.PrefetchScalarGridSpec(
        num_scalar_prefetch=1,
        grid=(nb,),
        in_specs=in_specs,
        out_specs=[pl.BlockSpec((seq, FOX_W), lambda b, pt: (b, 0)),
                   pl.BlockSpec((seq, LANES), lambda b, pt: (b, 0))],
    )
    return pl.pallas_call(
        functools.partial(_fox_decode_kernel, n_pages=n_pages, page=page, seq=seq),
        grid_spec=grid_spec,
        out_shape=[jax.ShapeDtypeStruct((n, FOX_W), F32),
                   jax.ShapeDtypeStruct((n, LANES), F32)],
        compiler_params=_cparams("arbitrary"),
        name="fox_decode",
    )(page_table, z, z, z, z, *([cache_kt] * n_pages), *([cache_vt] * n_pages), *([cache_lft] * n_pages))


def _mix_kernel(x_ref, ya_ref, yb_ref, ga_ref, gb_ref, woa_ref, wob_ref, wo_ref, o_ref):
    ya = _dot_f32(ya_ref[...].astype(BF16), woa_ref[...])
    yb = _dot_f32(yb_ref[...].astype(BF16), wob_ref[...])
    merged = _sigmoid(ga_ref[...]) * ya + _sigmoid(gb_ref[...]) * yb
    o_ref[...] = x_ref[...] + _dot_f32(merged.astype(BF16), wo_ref[...])


def _mix(x, ya, yb, z, woa, wob, wo):
    n, d = x.shape
    assert d == C_GB - C_GA
    tm = _pick_tile(n, 704)
    row = lambda w: pl.BlockSpec((tm, w), lambda i: (i, 0))
    full = lambda a: pl.BlockSpec(a.shape, lambda i: (0, 0))
    return pl.pallas_call(
        _mix_kernel,
        grid=(n // tm,),
        in_specs=[row(d), row(HG_W), row(FOX_W),
                  pl.BlockSpec((tm, d), lambda i: (i, C_GA // d)),
                  pl.BlockSpec((tm, d), lambda i: (i, C_GB // d)),
                  full(woa), full(wob), full(wo)],
        out_specs=row(d),
        out_shape=jax.ShapeDtypeStruct((n, d), F32),
        compiler_params=_cparams("parallel"),
        name="mix_out",
    )(x, ya, yb, z, z, woa, wob, wo)


def _ffn_kernel(x_ref, g_ref, wa_ref, wg_ref, cw_ref, cb_ref, wd_ref, ext_ref, o_ref, at_ref,
                h_scr, carry_scr, *, seq, tiles_per_seq):
    i, j = pl.program_id(0), pl.program_id(1)
    tm = x_ref.shape[0]

    @pl.when(j == 0)
    def _():
        x = x_ref[...]
        h_scr[...] = _rmsnorm(x, g_ref[...]).astype(BF16)
        o_ref[...] = x

    h = h_scr[...]
    a = _dot_f32(h, wa_ref[...])
    gate = _dot_f32(h, wg_ref[...])
    row = _iota(a.shape, 0)
    r1 = pltpu.roll(a, 1, 0)
    r2 = pltpu.roll(a, 2, 0)
    if tiles_per_seq:
        @pl.when((i == 0) & (j == 0))
        def _():
            carry_scr[...] = jnp.zeros_like(carry_scr)

        prev = carry_scr[j]
        seq_start = i % tiles_per_seq == 0
        p1 = jnp.where(seq_start, 0.0, prev[7:8])
        p2 = jnp.where(seq_start, 0.0, prev[6:7])
        a1 = jnp.where(row == 0, p1, r1)
        a2 = jnp.where(row == 0, p2, jnp.where(row == 1, p1, r2))
        carry_scr[j] = a[tm - 8:tm]
    else:
        ext = ext_ref[...]
        pos = row % seq
        a1 = jnp.where(pos == 0, pltpu.roll(ext, tm - 1, 0), r1)
        a2 = jnp.where(pos < 2, ext, r2)
    cw = cw_ref[...]
    conv = cb_ref[...] + cw[0:1] * a2 + cw[1:2] * a1 + cw[2:3] * a
    act = conv * _sigmoid(conv) * gate
    o_ref[...] += _dot_f32(act.astype(BF16), wd_ref[...])
    at_ref[...] = a[tm - at_ref.shape[0]:]


def _ffn(x, g, w_up, conv_w, conv_b, w_down, ext, seq):
    n, d = x.shape
    d_ff = w_down.shape[0]
    tf = 256
    assert d_ff % tf == 0
    nj = d_ff // tf
    if ext is None:
        tm = _pick_tile(seq, 1040)
        tiles_per_seq = seq // tm
        out_rows = 8
        ext = jnp.zeros((8, d_ff), F32)
        ext_spec = pl.BlockSpec((8, tf), lambda i, j: (0, j))
    else:
        tm = _pick_tile(n, 1040)
        assert tm % seq == 0 and seq == 8
        tiles_per_seq = 0
        out_rows = tm
        ext_spec = pl.BlockSpec((tm, tf), lambda i, j: (i, j))
    nt = n // tm
    out, a_tail = pl.pallas_call(
        functools.partial(_ffn_kernel, seq=seq, tiles_per_seq=tiles_per_seq),
        grid=(nt, nj),
        in_specs=[
            pl.BlockSpec((tm, d), lambda i, j: (i, 0)),
            pl.BlockSpec((1, d), lambda i, j: (0, 0)),
            pl.BlockSpec((d, tf), lambda i, j: (0, j)),
            pl.BlockSpec((d, tf), lambda i, j: (0, j + nj)),
            pl.BlockSpec((CONV_W, tf), lambda i, j: (0, j)),
            pl.BlockSpec((1, tf), lambda i, j: (0, j)),
            pl.BlockSpec((tf, d), lambda i, j: (j, 0)),
            ext_spec,
        ],
        out_specs=[pl.BlockSpec((tm, d), lambda i, j: (i, 0)),
                   pl.BlockSpec((out_rows, tf), lambda i, j: (i, j))],
        out_shape=[jax.ShapeDtypeStruct((n, d), F32),
                   jax.ShapeDtypeStruct((nt * out_rows, d_ff), F32)],
        scratch_shapes=[pltpu.VMEM((tm, d), BF16), pltpu.VMEM((nj, 8, tf), F32)],
        compiler_params=_cparams("arbitrary", "arbitrary"),
        name="conv_ffn",
    )(x, g.reshape(1, d), w_up, w_up, conv_w, conv_b.reshape(1, d_ff), w_down, ext)
    return out, a_tail, tiles_per_seq


def _norm_kernel(x_ref, g_ref, o_ref):
    o_ref[...] = _rmsnorm(x_ref[...], g_ref[...])


def _final_norm(x, g):
    n, d = x.shape
    tm = _pick_tile(n, 1040)
    return pl.pallas_call(
        _norm_kernel,
        grid=(n // tm,),
        in_specs=[pl.BlockSpec((tm, d), lambda i: (i, 0)), pl.BlockSpec((1, d), lambda i: (0, 0))],
        out_specs=pl.BlockSpec((tm, d), lambda i: (i, 0)),
        out_shape=jax.ShapeDtypeStruct((n, d), F32),
        compiler_params=_cparams("parallel"),
        name="final_norm",
    )(x, g.reshape(1, d))


def _norm_tail_kernel(x_ref, g_ref, o_ref):
    skip = x_ref.shape[1] - o_ref.shape[1]
    o_ref[0] = _rmsnorm(x_ref[0, skip:, :], g_ref[...])


def _final_norm_tail(x3, g, keep):
    nb, seq, d = x3.shape
    assert (seq - keep) % 8 == 0
    return pl.pallas_call(
        _norm_tail_kernel,
        grid=(nb,),
        in_specs=[pl.BlockSpec((1, seq, d), lambda b: (b, 0, 0)), pl.BlockSpec((1, d), lambda b: (0, 0))],
        out_specs=pl.BlockSpec((1, keep, d), lambda b: (b, 0, 0)),
        out_shape=jax.ShapeDtypeStruct((nb, keep, d), F32),
        compiler_params=_cparams("parallel"),
        name="final_norm_tail",
    )(x3, g.reshape(1, d))


def _reorder_in_proj(w_in, b_in):
    n_hf = 4 * HG_W + 3 * FOX_W
    pad = LANES - FOX_HEADS

    def reorder(a):
        parts = [a[..., n_hf + FOX_HEADS:], a[..., :n_hf], a[..., n_hf:n_hf + FOX_HEADS],
                 jnp.zeros(a.shape[:-1] + (pad,), a.dtype)]
        return jnp.concatenate(parts, axis=-1)

    return reorder(w_in).astype(BF16), reorder(b_in)


def kernel(x_prompt, x_sample, cache_k, cache_v, cache_logf, state_hgrn, state_conv, page_table, meta_tokens, norm1, norm2, norm_f, w_in, b_in, hg_lower_bounds, hg_norm, w_oa, w_ob, w_o, w_up, conv_w, conv_b, w_down):
    nb, s_len, d = x_prompt.shape
    seq_p = s_len + N_META
    db, seq_s, _ = x_sample.shape
    depth = w_in.shape[0]
    d_ff = w_down.shape[1]
    n_pool, page = cache_k.shape[1], cache_k.shape[2]
    assert w_in.shape[2] == 4 * HG_W + 3 * FOX_W + FOX_HEADS + 2 * d and 2 * d == C_ZQ

    w_in_r, b_in_r = _reorder_in_proj(w_in, b_in)
    w_oa, w_ob, w_o, w_up, w_down = (w.astype(BF16) for w in (w_oa, w_ob, w_o, w_up, w_down))
    cache_kt = jnp.transpose(cache_k, (0, 1, 3, 4, 2))
    cache_vt = jnp.transpose(cache_v, (0, 1, 3, 4, 2))
    cache_lft = jnp.swapaxes(cache_logf, 2, 3)
    lbraw = hg_lower_bounds.astype(F32)

    meta = jnp.broadcast_to(meta_tokens.astype(x_prompt.dtype)[None], (nb, N_META, d))
    xp = jnp.concatenate([meta, x_prompt], axis=1).reshape(nb * seq_p, d)
    xs = x_sample.reshape(db * seq_s, d)

    outs = [[] for _ in range(5)]
    kv_p = kv_s = hg_s = None
    for l in range(depth):
        z = _inproj(xp, norm1[l], w_in_r[l], b_in_r[l])
        z3 = z.reshape(nb, seq_p, NZ)
        *kv_p, lf, c = _fox_prep(z3, kv_p, l, depth)
        y_hg, s_new = _hgrn_prompt(z3, lbraw, hg_norm[l], l)
        o_fox = _fox_attn(z3, c)
        xp = _mix(xp, y_hg.reshape(nb * seq_p, HG_W), o_fox.reshape(nb * seq_p, FOX_W), z, w_oa[l], w_ob[l], w_o[l])
        xp, a_tail, tps = _ffn(xp, norm2[l], w_up[l], conv_w[l], conv_b[l], w_down[l], None, seq_p)
        outs[0].append(lf[:, :, :FOX_HEADS])
        outs[1].append(s_new)
        outs[2].append(a_tail.reshape(nb, tps, 8, d_ff)[:, tps - 1, 8 - (CONV_W - 1):])
        z, *kv_s = _inproj(xs, norm1[l], w_in_r[l], b_in_r[l], depth, kv_s, l)
        y_hg, hg_s = _hgrn_sample(z, state_hgrn, lbraw, hg_norm[l], l, seq_s, hg_s)
        o_fox, lf = _fox_decode(z, cache_kt, cache_vt, cache_lft, page_table, l, seq_s)
        xs = _mix(xs, y_hg, o_fox, z, w_oa[l], w_ob[l], w_o[l])
        ext = jnp.pad(state_conv[l], ((0, 0), (0, seq_s - (CONV_W - 1)), (0, 0))).reshape(db * seq_s, d_ff)
        xs, a_all, _ = _ffn(xs, norm2[l], w_up[l], conv_w[l], conv_b[l], w_down[l], ext, seq_s)
        outs[3].append(lf.reshape(db, seq_s, LANES)[:, :, :FOX_HEADS])
        outs[4].append(a_all.reshape(db, seq_s, d_ff)[:, seq_s - (CONV_W - 1):])
    y_prompt = _final_norm_tail(xp.reshape(nb, seq_p, d), norm_f, s_len)
    y_sample = _final_norm(xs, norm_f).reshape(db, seq_s, d)
    lf_p, hg_p, cv_p, lf_s, cv_s = (jnp.stack(o) for o in outs)
    k_p, v_p = (jnp.transpose(a.reshape(depth, nb, FOX_HEADS, FOX_DH, seq_p), (0, 1, 4, 2, 3)) for a in kv_p)
    k_s, v_s = (a.reshape(depth, db, seq_s, FOX_HEADS, FOX_DH) for a in kv_s)
    return (y_prompt, y_sample, k_p, v_p, lf_p, hg_p, cv_p, k_s, v_s, lf_s, hg_s, cv_s)
```

```python
import functools

import jax
import jax.numpy as jnp
from jax import lax
from jax.experimental import pallas as pl
from jax.experimental.pallas import tpu as pltpu

F32 = jnp.float32
BF16 = jnp.bfloat16
EPS = 1e-6
NEG_INF = float("-inf")
LOG2E = 1.4426950408889634

N_META = 16
HG_HEADS = 4
HG_DK = 128
FOX_HEADS = 8
FOX_DH = 64
CONV_W = 3

LANES = 128
HG_EXACT = 8
HG_GROUP = 4
FOX_PRUNE = 140.0
VMEM_LIMIT = 56 * 1024 * 1024

HG_W = HG_HEADS * HG_DK
FOX_W = FOX_HEADS * FOX_DH
C_GA = 0
C_GB = 1024
C_ZQ = 2048
C_ZF = C_ZQ + HG_W
C_ZI = C_ZF + HG_W
C_ZG = C_ZI + HG_W
C_FQ = C_ZG + HG_W
C_FK = C_FQ + FOX_W
C_FV = C_FK + FOX_W
C_FF = C_FV + FOX_W
NZ = C_FF + LANES


def _cparams(*sem):
    return pltpu.CompilerParams(dimension_semantics=sem, vmem_limit_bytes=VMEM_LIMIT)


def _pick_tile(n, cap):
    best = None
    for d in range(8, min(n, cap) + 1, 8):
        if n % d == 0:
            best = d
    assert best is not None, (n, cap)
    return best


def _sigmoid(x):
    return 0.5 * jnp.tanh(0.5 * x) + 0.5


def _log_sigmoid(x):
    return jnp.minimum(x, 0.0) - jnp.log1p(jnp.exp(-jnp.abs(x)))


def _split3(x):
    hi = x.astype(BF16)
    r = x - hi.astype(F32)
    mid = r.astype(BF16)
    r = r - mid.astype(F32)
    return hi, mid, r.astype(BF16)


def _dot_f32(a, b, dims=None):
    if dims is None:
        return jnp.dot(a, b, preferred_element_type=F32)
    return lax.dot_general(a, b, (dims, ((), ())), preferred_element_type=F32)


NT = ((1,), (1,))


def _mask_dot_left(m, x):
    return sum(_dot_f32(m, p) for p in _split3(x))


def _mask_dot_right(x, m):
    return sum(_dot_f32(p, m) for p in _split3(x))


def _rmsnorm(x, g):
    return x * lax.rsqrt(jnp.mean(x * x, axis=-1, keepdims=True) + EPS) * g


def _iota(shape, dim):
    return lax.broadcasted_iota(jnp.int32, shape, dim)


def _inproj_kernel(x_ref, g_ref, w_ref, b_ref, *rest, tn, with_kv):
    if with_kv:
        z_ref, k_ref, v_ref, h_scr = rest[2:]
    else:
        z_ref, h_scr = rest
    j = pl.program_id(1)

    @pl.when(j == 0)
    def _():
        h_scr[...] = _rmsnorm(x_ref[...], g_ref[...]).astype(BF16)

    z = _dot_f32(h_scr[...], w_ref[...]) + b_ref[...]
    z_ref[...] = z

    if with_kv:
        @pl.when(j == C_FK // tn)
        def _():
            k_ref[...] = z[:, C_FK % tn:C_FK % tn + FOX_W]
            v_ref[...] = z[:, C_FV % tn:C_FV % tn + FOX_W]


def _inproj(x, g, w, b, kv=None, layer=0):
    n, d = x.shape
    tm = _pick_tile(n, 1040)
    tn = 1152
    assert NZ % tn == 0 and C_FK // tn == (C_FV + FOX_W - 1) // tn
    in_specs = [
        pl.BlockSpec((tm, d), lambda i, j: (i, 0)),
        pl.BlockSpec((1, d), lambda i, j: (0, 0)),
        pl.BlockSpec((d, tn), lambda i, j: (0, j)),
        pl.BlockSpec((1, tn), lambda i, j: (0, j)),
    ]
    out_specs = [pl.BlockSpec((tm, tn), lambda i, j: (i, j))]
    out_shape = [jax.ShapeDtypeStruct((n, NZ), F32)]
    args = [x, g.reshape(1, d), w, b.reshape(1, NZ)]
    aliases = {}
    if kv is not None:
        in_specs += [pl.BlockSpec(memory_space=pl.ANY)] * 2
        out_specs += [pl.BlockSpec((None, tm, FOX_W), lambda i, j: (layer, i, 0))] * 2
        out_shape += [jax.ShapeDtypeStruct(a.shape, F32) for a in kv]
        args += list(kv)
        aliases = {4: 1, 5: 2}
    out = pl.pallas_call(
        functools.partial(_inproj_kernel, tn=tn, with_kv=kv is not None),
        grid=(n // tm, NZ // tn),
        in_specs=in_specs,
        out_specs=out_specs,
        out_shape=out_shape,
        input_output_aliases=aliases,
        scratch_shapes=[pltpu.VMEM((tm, d), BF16)],
        compiler_params=_cparams("parallel", "arbitrary"),
        name="inproj",
    )(*args)
    return out if kv is not None else out[0]


def _hgrn_lower_bound(lbraw, layer):
    e = jnp.exp(lbraw - jnp.max(lbraw, axis=0, keepdims=True))
    sm = e / jnp.sum(e, axis=0, keepdims=True)
    lb = jnp.zeros_like(sm[0:1])
    for i in range(1, layer + 1):
        lb = lb + sm[i:i + 1]
    return lb


def _hgrn_gates(zq, zf, zi, lb):
    q = zq * _sigmoid(zq)
    a = jnp.log(lb)
    b = jnp.log1p(-lb) + _log_sigmoid(zf)
    logf = jnp.maximum(a, b) + jnp.log1p(jnp.exp(-jnp.abs(a - b)))
    k = (1.0 - lb) * _sigmoid(-zf)
    return q, k, zi, logf


def _hgrn_diag(q, k, b, v):
    n = q.shape[0]
    rowid = _iota(q.shape, 0)
    acc = jnp.zeros_like(v)
    for s in range(n):
        d = jnp.where(rowid >= s, b - b[s:s + 1], NEG_INF)
        w = q * k[s:s + 1] * jnp.exp2(d)
        acc = acc + jnp.sum(w, axis=-1, keepdims=True) * v[s:s + 1]
    return acc


def _hgrn_head_out(o, zg, gain):
    o = o * lax.rsqrt(jnp.mean(o * o, axis=-1, keepdims=True) + EPS)
    return o * gain * (zg * _sigmoid(zg))


def _pad_rows(x, rows):
    if x.shape[0] == rows:
        return x
    return jnp.concatenate([x, jnp.zeros((rows - x.shape[0], x.shape[1]), x.dtype)], axis=0)


def _hgrn_prompt_kernel(zq_ref, zf_ref, zi_ref, zg_ref, lbraw_ref, gain_ref, y_ref, s_ref, st_scr,
                        *, layer, seq):
    ck = LANES
    n_full, tail = divmod(seq, ck)
    assert tail % HG_EXACT == 0
    lb_all = _hgrn_lower_bound(lbraw_ref[...], layer)
    gain_all = gain_ref[...]
    row = _iota((ck, ck), 0)
    col = _iota((ck, ck), 1)
    ltri = jnp.where(col <= row, 1.0, 0.0).astype(BF16)
    levels = []
    half = HG_EXACT
    while half < ck:
        levels.append((half, row % (2 * half) >= half, row // (2 * half) == col // (2 * half)))
        half *= 2
    st_scr[...] = jnp.zeros_like(st_scr)

    def chunk(r0, n_rows):
        for g in range(HG_GROUP):
            head_chunk(r0, n_rows, g, slice(HG_DK * g, HG_DK * (g + 1)))

    def head_chunk(r0, n_rows, g, hs):
        sl = pl.ds(r0, n_rows)
        lb, gain = lb_all[:, hs], gain_all[:, hs]
        q, k, v, logf = _hgrn_gates(zq_ref[0, sl, hs], zf_ref[0, sl, hs], zi_ref[0, sl, hs], lb)
        q, k, v, logf = (_pad_rows(a, ck) for a in (q, k, v, logf))
        b = _mask_dot_left(ltri, logf * LOG2E)
        bend = b[ck - 1:ck]
        a = jnp.zeros((ck, ck), F32)
        for half, upper, same_blk in levels:
            blk = 2 * half
            bmid = jnp.concatenate(
                [jnp.broadcast_to(b[blk * i + half - 1:blk * i + half], (blk, ck)) for i in range(ck // blk)],
                axis=0)
            q2 = jnp.where(upper, q * jnp.exp2(jnp.minimum(b - bmid, 0.0)), 0.0).astype(BF16)
            k2 = jnp.where(upper, 0.0, k * jnp.exp2(jnp.minimum(bmid - b, 0.0))).astype(BF16)
            a = a + jnp.where(same_blk, _dot_f32(q2, k2, NT), 0.0)
        st = st_scr[g]
        o = _dot_f32(a.astype(BF16), v.astype(BF16))
        o = o + _dot_f32((q * jnp.exp2(b)).astype(BF16), st.astype(BF16), NT)
        diag = [_hgrn_diag(*(x[HG_EXACT * i:HG_EXACT * (i + 1)] for x in (q, k, b, v)))
                for i in range(n_rows // HG_EXACT)]
        o = o[:n_rows] + jnp.concatenate(diag, axis=0)
        st_scr[g] = st * jnp.exp2(bend) + _dot_f32(v.T.astype(BF16), (k * jnp.exp2(bend - b)).astype(BF16))
        y_ref[0, sl, hs] = _hgrn_head_out(o, zg_ref[0, sl, hs], gain).astype(y_ref.dtype)

    def body(c, carry):
        chunk(pl.multiple_of(c * ck, ck), ck)
        return carry

    lax.fori_loop(0, n_full, body, 0)
    if tail:
        chunk(n_full * ck, tail)
    for g in range(HG_GROUP):
        s_ref[0, g] = st_scr[g].T


def _hgrn_prompt(z3, lbraw, gain, layer):
    nb, seq, _ = z3.shape
    gw = HG_GROUP * HG_DK

    def zspec(c0):
        return pl.BlockSpec((1, seq, gw), lambda b, h, c0=c0: (b, 0, c0 // gw + h))

    return pl.pallas_call(
        functools.partial(_hgrn_prompt_kernel, layer=layer, seq=seq),
        grid=(nb, HG_HEADS // HG_GROUP),
        in_specs=[zspec(C_ZQ), zspec(C_ZF), zspec(C_ZI), zspec(C_ZG),
                  pl.BlockSpec((lbraw.shape[0], gw), lambda b, h: (0, h)),
                  pl.BlockSpec((1, gw), lambda b, h: (0, h))],
        out_specs=[pl.BlockSpec((1, seq, gw), lambda b, h: (b, 0, h)),
                   pl.BlockSpec((1, HG_GROUP, HG_DK, HG_DK), lambda b, h: (b, h, 0, 0))],
        out_shape=[jax.ShapeDtypeStruct((nb, seq, HG_W), BF16),
                   jax.ShapeDtypeStruct((nb, HG_HEADS, HG_DK, HG_DK), F32)],
        scratch_shapes=[pltpu.VMEM((HG_GROUP, HG_DK, HG_DK), F32)],
        compiler_params=_cparams("parallel", "parallel"),
        name="hgrn_prompt",
    )(z3, z3, z3, z3, lbraw, gain.reshape(1, HG_W))


def _hgrn_sample_kernel(zq_ref, zf_ref, zi_ref, zg_ref, lbraw_ref, gain_ref, s0_ref, stack_hbm, y_ref, s_ref,
                        *, layer, seq):
    del stack_hbm
    rows = zq_ref.shape[0]
    lb_all = _hgrn_lower_bound(lbraw_ref[...], layer)
    row = _iota((rows, rows), 0)
    col = _iota((rows, rows), 1)
    ltri = jnp.where((row // seq == col // seq) & (col <= row), 1.0, 0.0).astype(BF16)
    for g in range(HG_HEADS):
        hs = slice(HG_DK * g, HG_DK * (g + 1))
        _hgrn_sample_head(zq_ref, zf_ref, zi_ref, zg_ref, gain_ref, s0_ref, y_ref, s_ref, lb_all[:, hs], ltri,
                          g, hs, seq)


def _hgrn_sample_head(zq_ref, zf_ref, zi_ref, zg_ref, gain_ref, s0_ref, y_ref, s_ref, lb, ltri, g, hs, seq):
    rows = zq_ref.shape[0]
    n_seq = rows // seq
    q, k, v, logf = _hgrn_gates(zq_ref[:, hs], zf_ref[:, hs], zi_ref[:, hs], lb)
    b = _mask_dot_left(ltri, logf * LOG2E)
    bend = jnp.concatenate(
        [jnp.broadcast_to(b[seq * s + seq - 1:seq * (s + 1)], (seq, LANES)) for s in range(n_seq)], axis=0)
    qt = (q * jnp.exp2(b)).astype(BF16)
    kt = k * jnp.exp2(bend - b)
    outs = []
    for s in range(n_seq):
        rs = slice(seq * s, seq * (s + 1))
        s0 = s0_ref[s, g]
        outs.append(_dot_f32(qt[rs], s0.astype(BF16)) + _hgrn_diag(q[rs], k[rs], b[rs], v[rs]))
        x = jnp.concatenate([kt[rs], jnp.exp2(bend[rs]), jnp.zeros((HG_DK - 2 * seq, LANES), F32)], axis=0)
        xt = x.T
        inc = _dot_f32(xt.astype(BF16), _pad_rows(v[rs], HG_DK).astype(BF16))
        s_ref[s, g] = xt[:, seq:seq + 1] * s0 + inc
    o = jnp.concatenate(outs, axis=0)
    y_ref[:, hs] = _hgrn_head_out(o, zg_ref[:, hs], gain_ref[:, hs]).astype(y_ref.dtype)


def _hgrn_sample(z, state, lbraw, gain, layer, seq, new_state):
    n = z.shape[0]
    rows = LANES
    assert n % rows == 0 and rows % seq == 0 and 2 * seq <= HG_DK
    sblk = rows // seq

    def zspec(c0):
        return pl.BlockSpec((rows, HG_W), lambda i, c0=c0: (i, c0 // HG_W))

    st_spec = pl.BlockSpec((None, sblk, HG_HEADS, HG_DK, HG_DK), lambda i: (layer, i, 0, 0, 0))
    in_specs = [zspec(C_ZQ), zspec(C_ZF), zspec(C_ZI), zspec(C_ZG),
                pl.BlockSpec(lbraw.shape, lambda i: (0, 0)),
                pl.BlockSpec((1, HG_W), lambda i: (0, 0)),
                st_spec,
                pl.BlockSpec(memory_space=pl.ANY)]
    args = [z, z, z, z, lbraw, gain.reshape(1, HG_W), state, new_state]
    return pl.pallas_call(
        functools.partial(_hgrn_sample_kernel, layer=layer, seq=seq),
        grid=(n // rows,),
        in_specs=in_specs,
        out_specs=[pl.BlockSpec((rows, HG_W), lambda i: (i, 0)), st_spec],
        out_shape=[jax.ShapeDtypeStruct((n, HG_W), BF16),
                   jax.ShapeDtypeStruct(state.shape, F32)],
        input_output_aliases={7: 1},
        compiler_params=_cparams("parallel"),
        name="hgrn_sample",
    )(*args)


def _fox_gate(ff_ref, lf_ref, c_ref, seq):
    ck = LANES
    n_full, tail = divmod(seq, ck)
    ltri = jnp.where(_iota((ck, ck), 1) <= _iota((ck, ck), 0), 1.0, 0.0).astype(BF16)

    def chunk(r0, n_rows, carry):
        sl = pl.ds(r0, n_rows)
        lf = _log_sigmoid(ff_ref[0, sl, :])
        lf_ref[0, sl, :] = lf
        c = _mask_dot_left(ltri, _pad_rows(lf, ck)) + carry
        c_ref[0, sl, :] = c[:n_rows]
        return c[n_rows - 1:n_rows]

    carry = lax.fori_loop(0, n_full, lambda i, cr: chunk(pl.multiple_of(i * ck, ck), ck, cr),
                          jnp.zeros((1, LANES), F32))
    if tail:
        chunk(n_full * ck, tail, carry)


def _fox_attn_kernel(fq_ref, fk_ref, fv_ref, c_ref, o_ref, qa_scr, ka_scr, va_scr, r_scr, acc_scr,
                     qn_scr, kn_scr, *, seq, bq):
    padded = qa_scr.shape[1]
    nq = padded // bq
    ck = LANES
    n_full, tail = divmod(seq, ck)
    scale = FOX_DH ** -0.5 * LOG2E

    lane8 = _iota((8, LANES), 1)

    def max_sq_norm(x, own):
        return jnp.max(jnp.sum(jnp.where(own, x * x, 0.0), axis=-1, keepdims=True), axis=0, keepdims=True)

    def build(i, r0, n_rows):
        sl = pl.ds(r0, n_rows)
        cb = c_ref[0, sl, :]
        lane = _iota((n_rows, LANES), 1)
        qn = jnp.zeros((8, LANES), F32)
        kn = jnp.zeros((8, LANES), F32)
        for h in range(FOX_HEADS):
            hh = h % 2
            ps = slice(LANES * (h // 2), LANES * (h // 2 + 1))
            qf, kf, vf = fq_ref[0, sl, ps], fk_ref[0, sl, ps], fv_ref[0, sl, ps]
            ch = jnp.sum(jnp.where(lane == h, cb, 0.0), axis=-1, keepdims=True) * LOG2E
            hi, mid, lo = (p.astype(F32) for p in _split3(ch))
            own = (lane >= FOX_DH * hh) & (lane < FOX_DH * (hh + 1))
            f0 = FOX_DH * (1 - hh)
            ones_q = (lane >= f0 + 3) & (lane < f0 + 6)
            ones_k = (lane >= f0) & (lane < f0 + 3)
            qa = jnp.where(own, qf * scale,
                           jnp.where(lane == f0, hi, jnp.where(lane == f0 + 1, mid, jnp.where(
                               lane == f0 + 2, lo, jnp.where(ones_q, 1.0, 0.0)))))
            ka = jnp.where(own, kf,
                           jnp.where(ones_k, 1.0, jnp.where(lane == f0 + 3, -hi, jnp.where(
                               lane == f0 + 4, -mid, jnp.where(lane == f0 + 5, -lo, 0.0)))))
            qa_scr[h, sl, :] = qa.astype(BF16)
            ka_scr[h, sl, :] = ka.astype(BF16)
            va_scr[h, sl, :] = jnp.where(own, vf, 1.0).astype(BF16)
            qn = jnp.where(lane8 == h, max_sq_norm(qf, own), qn)
            kn = jnp.where(lane8 == h, max_sq_norm(kf, own), kn)
        qn_scr[i] = qn
        kn_scr[i] = kn

    def build_body(i, carry):
        build(i, pl.multiple_of(i * ck, ck), ck)
        return carry

    lax.fori_loop(0, n_full, build_body, 0)
    if tail:
        build(n_full, n_full * ck, tail)
    kmax = jnp.max(kn_scr[...], axis=0)
    if padded > seq:
        zpad = jnp.zeros((padded - seq, LANES), BF16)
        for h in range(FOX_HEADS):
            qa_scr[h, seq:padded, :] = zpad
            ka_scr[h, seq:padded, :] = zpad
            va_scr[h, seq:padded, :] = zpad

    kc = 2 * bq

    def qblock(q0, n_chunks, tail_k0, tail_w, rows):
        qa = [qa_scr[h, pl.ds(q0, rows), :] for h in range(FOX_HEADS)]
        tail_mask = _iota((rows, tail_w), 1) <= _iota((rows, tail_w), 0) + (tail_w - bq)
        lane_q = _iota((rows, LANES), 1)

        def scores(hh, k0, width, mask):
            s = _dot_f32(qa[hh], ka_scr[hh, pl.ds(k0, width), :], NT)
            return s if mask is None else jnp.where(mask, s, NEG_INF)

        def step(k0, width, mask):
            for hh in range(FOX_HEADS):
                s = scores(hh, k0, width, mask)
                m_old = r_scr[hh, :rows]
                blk = s[:, :LANES]
                for t in range(1, width // LANES):
                    blk = jnp.maximum(blk, s[:, LANES * t:LANES * (t + 1)])
                m_new = jnp.maximum(m_old, jnp.broadcast_to(jnp.max(blk, axis=-1, keepdims=True), (rows, LANES)))
                p = jnp.exp2(s - jnp.concatenate([m_new] * (width // LANES), axis=1))
                acc_scr[hh, :rows] = jnp.exp2(m_old - m_new) * acc_scr[hh, :rows] + _dot_f32(
                    p.astype(BF16), va_scr[hh, pl.ds(k0, width), :])
                r_scr[hh, :rows] = m_new

        r_scr[...] = jnp.full(r_scr.shape, NEG_INF, F32)
        acc_scr[...] = jnp.zeros_like(acc_scr)
        step(tail_k0, tail_w, tail_mask)
        i0 = q0 // ck
        qmax = qn_scr[i0]
        for t in range(1, -(-rows // ck)):
            qmax = jnp.maximum(qmax, qn_scr[i0 + t])
        mmin = jnp.zeros((8, LANES), F32)
        for h in range(FOX_HEADS):
            mmin = jnp.where(lane8 == h, jnp.min(r_scr[h, :rows], axis=0, keepdims=True), mmin)
        thr = (mmin - FOX_PRUNE - jnp.sqrt(qmax * kmax) * (scale * 1.05))[0:1]
        c_q = c_ref[0, pl.ds(q0, 1), :]

        def body(jj, carry):
            k0 = pl.multiple_of((n_chunks - 1 - jj) * kc, kc)
            c_k = c_ref[0, pl.ds(k0 + kc - 1, 1), :]
            reach = jnp.where(lane8[0:1] < FOX_HEADS, (c_q - c_k) * LOG2E - thr, NEG_INF)

            @pl.when(jnp.max(reach) >= 0.0)
            def _():
                step(k0, kc, None)
            return carry

        lax.fori_loop(0, n_chunks, body, 0)
        outs = [acc_scr[h, :rows] / pltpu.roll(acc_scr[h, :rows], FOX_DH, 1) for h in range(FOX_HEADS)]
        for pr in range(FOX_HEADS // 2):
            o = jnp.where(lane_q < FOX_DH, outs[2 * pr], outs[2 * pr + 1])
            o_ref[0, pl.ds(q0, rows), LANES * pr:LANES * (pr + 1)] = o.astype(o_ref.dtype)

    def even_block(u, rows=bq):
        q0 = pl.multiple_of(u * kc, kc) if not isinstance(u, int) else u * kc
        qblock(q0, u, q0, bq, rows)

    def odd_block(u, rows=bq):
        k0 = pl.multiple_of(u * kc, kc) if not isinstance(u, int) else u * kc
        qblock(k0 + bq, u, k0, kc, rows)

    def pair_body(u, carry):
        even_block(u)
        odd_block(u)
        return carry

    n_whole = seq // bq
    lax.fori_loop(0, n_whole // 2, pair_body, 0)
    rest = [(i, bq) for i in range(n_whole - n_whole % 2, n_whole)]
    if n_whole < nq:
        rest.append((n_whole, seq - n_whole * bq))
    for i, rows in rest:
        (odd_block if i % 2 else even_block)(i // 2, rows)


def _fox_attn(z3, c):
    nb, seq, _ = z3.shape
    bq = 256
    padded = -(-seq // bq) * bq

    def zspec(c0):
        return pl.BlockSpec((1, seq, FOX_W), lambda b, c0=c0: (b, 0, c0 // FOX_W))

    return pl.pallas_call(
        functools.partial(_fox_attn_kernel, seq=seq, bq=bq),
        grid=(nb,),
        in_specs=[zspec(C_FQ), zspec(C_FK), zspec(C_FV),
                  pl.BlockSpec((1, seq, LANES), lambda b: (b, 0, 0))],
        out_specs=pl.BlockSpec((1, seq, FOX_W), lambda b: (b, 0, 0)),
        out_shape=jax.ShapeDtypeStruct((nb, seq, FOX_W), BF16),
        scratch_shapes=[pltpu.VMEM((FOX_HEADS, padded, LANES), BF16)] * 3
        + [pltpu.VMEM((FOX_HEADS, bq, LANES), F32)] * 2
        + [pltpu.VMEM((-(-seq // LANES), 8, LANES), F32)] * 2,
        compiler_params=_cparams("parallel"),
        name="fox_attn",
    )(z3, z3, z3, c)


def _fox_prep_kernel(fk_ref, fv_ref, ff_ref, k_hbm, v_hbm, kt_ref, vt_ref, lf_ref, c_ref, tail_scr, *, seq):
    del k_hbm, v_hbm
    _fox_gate(ff_ref, lf_ref, c_ref, seq)
    n_full, tail = divmod(seq, LANES)
    for src, dst in ((fk_ref, kt_ref), (fv_ref, vt_ref)):
        for i in range(n_full):
            dst[:, LANES * i:LANES * (i + 1)] = src[0, LANES * i:LANES * (i + 1), :].T
        if tail:
            tail_scr[...] = _pad_rows(src[0, n_full * LANES:seq, :], LANES).T
            dst[:, n_full * LANES:seq] = tail_scr[:, :tail]


def _fox_prep(z3, stacks, layer):
    nb, seq, _ = z3.shape
    kv_spec = pl.BlockSpec((None, None, FOX_W, seq), lambda b: (layer, b, 0, 0))
    g_spec = pl.BlockSpec((1, seq, LANES), lambda b: (b, 0, 0))
    in_specs = [pl.BlockSpec((1, seq, FOX_W), lambda b: (b, 0, C_FK // FOX_W)),
                pl.BlockSpec((1, seq, FOX_W), lambda b: (b, 0, C_FV // FOX_W)),
                pl.BlockSpec((1, seq, LANES), lambda b: (b, 0, C_FF // LANES)),
                pl.BlockSpec(memory_space=pl.ANY), pl.BlockSpec(memory_space=pl.ANY)]
    kv_shape = jax.ShapeDtypeStruct(stacks[0].shape, F32)
    g_shape = jax.ShapeDtypeStruct((nb, seq, LANES), F32)
    return pl.pallas_call(
        functools.partial(_fox_prep_kernel, seq=seq),
        grid=(nb,),
        in_specs=in_specs,
        out_specs=[kv_spec, kv_spec, g_spec, g_spec],
        out_shape=[kv_shape, kv_shape, g_shape, g_shape],
        input_output_aliases={3: 0, 4: 1},
        scratch_shapes=[pltpu.VMEM((FOX_W, LANES), F32)],
        compiler_params=_cparams("parallel"),
        name="fox_prep",
    )(z3, z3, z3, *stacks)


def _fox_decode_kernel(pt_ref, fq_ref, fk_ref, fv_ref, ff_ref, *rest, n_pages, page, seq):
    del pt_ref
    k_refs, v_refs, lf_refs = rest[:n_pages], rest[n_pages:2 * n_pages], rest[2 * n_pages:3 * n_pages]
    o_ref, lfo_ref = rest[3 * n_pages:]
    assert page == LANES
    rows = FOX_HEADS * seq
    lfn = _log_sigmoid(ff_ref[...])
    lfo_ref[...] = lfn
    lfn = jnp.where(_iota(lfn.shape, 1) < FOX_HEADS, lfn, 0.0)
    lfn_t = _pad_rows(lfn, LANES).T[:FOX_HEADS]
    stack = jnp.concatenate([r[0, 0] for r in lf_refs] + [lfn_t], axis=0)
    ustrict = jnp.where(_iota((LANES, LANES), 0) > _iota((LANES, LANES), 1), 1.0, 0.0).astype(BF16)
    within = _mask_dot_right(stack, ustrict)
    tot = jnp.sum(stack, axis=-1, keepdims=True)
    g = [None] * (n_pages + 1)
    run = jnp.zeros((FOX_HEADS, 1), F32)
    for p in reversed(range(n_pages + 1)):
        rs = slice(FOX_HEADS * p, FOX_HEADS * (p + 1))
        g[p] = within[rs] + run
        run = run + tot[rs]
    gq = _pad_rows(g[n_pages], LANES).T[:seq]
    lane = _iota((seq, LANES), 1)
    gq_col = jnp.concatenate(
        [jnp.sum(jnp.where(lane == h, gq, 0.0), axis=-1, keepdims=True) for h in range(FOX_HEADS)], axis=0)

    def bias(gp):
        return jnp.concatenate(
            [jnp.broadcast_to(gp[h:h + 1], (seq, page)) for h in range(FOX_HEADS)], axis=0) - gq_col

    own = _iota((rows, FOX_W), 0) // seq == _iota((rows, FOX_W), 1) // FOX_DH
    q = fq_ref[...] * (FOX_DH ** -0.5)
    q_bd = jnp.where(own, jnp.concatenate([q] * FOX_HEADS, axis=0), 0.0).astype(BF16)
    s_blocks = []
    for p in range(n_pages):
        kt = k_refs[p][0, 0].reshape(FOX_W, page).astype(BF16)
        s_blocks.append(_dot_f32(q_bd, kt) + bias(g[p]))
    s = _dot_f32(q_bd, _pad_rows(fk_ref[...], page).astype(BF16), NT) + bias(g[n_pages])
    s_blocks.append(jnp.where(_iota((rows, page), 1) <= _iota((rows, page), 0) % seq, s, NEG_INF))
    mx = s_blocks[0]
    for s in s_blocks[1:]:
        mx = jnp.maximum(mx, s)
    m = jnp.max(mx, axis=-1, keepdims=True)
    wsum = jnp.zeros((rows, page), F32)
    acc = jnp.zeros((rows, FOX_W), F32)
    for p in range(n_pages + 1):
        pe = jnp.exp(s_blocks[p] - m)
        wsum = wsum + pe
        if p < n_pages:
            acc = acc + _dot_f32(pe.astype(BF16), v_refs[p][0, 0].reshape(FOX_W, page).astype(BF16), NT)
        else:
            acc = acc + _dot_f32(pe.astype(BF16), _pad_rows(fv_ref[...], page).astype(BF16))
    acc = jnp.where(own, acc / jnp.sum(wsum, axis=-1, keepdims=True), 0.0)
    o = acc[:seq]
    for h in range(1, FOX_HEADS):
        o = o + acc[seq * h:seq * (h + 1)]
    o_ref[...] = o


def _fox_decode(z, cache_kt, cache_vt, cache_lft, page_table, layer, seq):
    n = z.shape[0]
    nb, n_pages = page_table.shape
    page = cache_lft.shape[-1]
    assert nb * seq == n and seq == 8

    def zspec(c0, w):
        return pl.BlockSpec((seq, w), lambda b, pt, c0=c0, w=w: (b, c0 // w))

    def pspec(shape, p):
        return pl.BlockSpec(shape, lambda b, pt, p=p: (layer, pt[b, p]) + (0,) * (len(shape) - 2))

    kv_shape = (1, 1, FOX_HEADS, FOX_DH, page)
    lf_shape = (1, 1, FOX_HEADS, page)
    in_specs = [zspec(C_FQ, FOX_W), zspec(C_FK, FOX_W), zspec(C_FV, FOX_W), zspec(C_FF, LANES)]
    in_specs += [pspec(kv_shape, p) for p in range(n_pages)]
    in_specs += [pspec(kv_shape, p) for p in range(n_pages)]
    in_specs += [pspec(lf_shape, p) for p in range(n_pages)]
    grid_spec = pltpu.PrefetchScalarGridSpec(
        num_scalar_prefetch=1,
        grid=(nb,),
        in_specs=in_specs,
        out_specs=[pl.BlockSpec((seq, FOX_W), lambda b, pt: (b, 0)),
                   pl.BlockSpec((seq, LANES), lambda b, pt: (b, 0))],
    )
    return pl.pallas_call(
        functools.partial(_fox_decode_kernel, n_pages=n_pages, page=page, seq=seq),
        grid_spec=grid_spec,
        out_shape=[jax.ShapeDtypeStruct((n, FOX_W), F32),
                   jax.ShapeDtypeStruct((n, LANES), F32)],
        compiler_params=_cparams("arbitrary"),
        name="fox_decode",
    )(page_table, z, z, z, z, *([cache_kt] * n_pages), *([cache_vt] * n_pages), *([cache_lft] * n_pages))


def _mix_kernel(x_ref, ya_ref, yb_ref, ga_ref, gb_ref, woa_ref, wob_ref, wo_ref, o_ref):
    ya = _dot_f32(ya_ref[...].astype(BF16), woa_ref[...])
    yb = _dot_f32(yb_ref[...].astype(BF16), wob_ref[...])
    merged = _sigmoid(ga_ref[...]) * ya + _sigmoid(gb_ref[...]) * yb
    o_ref[...] = x_ref[...] + _dot_f32(merged.astype(BF16), wo_ref[...])


def _mix(x, ya, yb, z, woa, wob, wo):
    n, d = x.shape
    assert d == C_GB - C_GA
    tm = _pick_tile(n, 704)
    row = lambda w: pl.BlockSpec((tm, w), lambda i: (i, 0))
    full = lambda a: pl.BlockSpec(a.shape, lambda i: (0, 0))
    return pl.pallas_call(
        _mix_kernel,
        grid=(n // tm,),
        in_specs=[row(d), row(HG_W), row(FOX_W),
                  pl.BlockSpec((tm, d), lambda i: (i, C_GA // d)),
                  pl.BlockSpec((tm, d), lambda i: (i, C_GB // d)),
                  full(woa), full(wob), full(wo)],
        out_specs=row(d),
        out_shape=jax.ShapeDtypeStruct((n, d), F32),
        compiler_params=_cparams("parallel"),
        name="mix_out",
    )(x, ya, yb, z, z, woa, wob, wo)


def _ffn_kernel(x_ref, g_ref, wa_ref, wg_ref, cw_ref, cb_ref, wd_ref, ext_ref, o_ref, at_ref,
                h_scr, carry_scr, *, seq, tiles_per_seq):
    i, j = pl.program_id(0), pl.program_id(1)
    tm = x_ref.shape[0]

    @pl.when(j == 0)
    def _():
        x = x_ref[...]
        h_scr[...] = _rmsnorm(x, g_ref[...]).astype(BF16)
        o_ref[...] = x

    h = h_scr[...]
    a = _dot_f32(h, wa_ref[...])
    gate = _dot_f32(h, wg_ref[...])
    row = _iota(a.shape, 0)
    r1 = pltpu.roll(a, 1, 0)
    r2 = pltpu.roll(a, 2, 0)
    if tiles_per_seq:
        @pl.when((i == 0) & (j == 0))
        def _():
            carry_scr[...] = jnp.zeros_like(carry_scr)

        prev = carry_scr[j]
        seq_start = i % tiles_per_seq == 0
        p1 = jnp.where(seq_start, 0.0, prev[7:8])
        p2 = jnp.where(seq_start, 0.0, prev[6:7])
        a1 = jnp.where(row == 0, p1, r1)
        a2 = jnp.where(row == 0, p2, jnp.where(row == 1, p1, r2))
        carry_scr[j] = a[tm - 8:tm]
    else:
        ext = ext_ref[...]
        pos = row % seq
        a1 = jnp.where(pos == 0, pltpu.roll(ext, tm - 1, 0), r1)
        a2 = jnp.where(pos < 2, ext, r2)
    cw = cw_ref[...]
    conv = cb_ref[...] + cw[0:1] * a2 + cw[1:2] * a1 + cw[2:3] * a
    act = conv * _sigmoid(conv) * gate
    o_ref[...] += _dot_f32(act.astype(BF16), wd_ref[...])
    at_ref[...] = a[tm - at_ref.shape[0]:]


def _ffn(x, g, w_up, conv_w, conv_b, w_down, ext, seq):
    n, d = x.shape
    d_ff = w_down.shape[0]
    tf = 256
    assert d_ff % tf == 0
    nj = d_ff // tf
    if ext is None:
        tm = _pick_tile(seq, 2064)
        tiles_per_seq = seq // tm
        out_rows = 8
        ext = jnp.zeros((8, d_ff), F32)
        ext_spec = pl.BlockSpec((8, tf), lambda i, j: (0, j))
    else:
        tm = _pick_tile(n, 1040)
        assert tm % seq == 0 and seq == 8
        tiles_per_seq = 0
        out_rows = tm
        ext_spec = pl.BlockSpec((tm, tf), lambda i, j: (i, j))
    nt = n // tm
    out, a_tail = pl.pallas_call(
        functools.partial(_ffn_kernel, seq=seq, tiles_per_seq=tiles_per_seq),
        grid=(nt, nj),
        in_specs=[
            pl.BlockSpec((tm, d), lambda i, j: (i, 0)),
            pl.BlockSpec((1, d), lambda i, j: (0, 0)),
            pl.BlockSpec((d, tf), lambda i, j: (0, j)),
            pl.BlockSpec((d, tf), lambda i, j: (0, j + nj)),
            pl.BlockSpec((CONV_W, tf), lambda i, j: (0, j)),
            pl.BlockSpec((1, tf), lambda i, j: (0, j)),
            pl.BlockSpec((tf, d), lambda i, j: (j, 0)),
            ext_spec,
        ],
        out_specs=[pl.BlockSpec((tm, d), lambda i, j: (i, 0)),
                   pl.BlockSpec((out_rows, tf), lambda i, j: (i, j))],
        out_shape=[jax.ShapeDtypeStruct((n, d), F32),
                   jax.ShapeDtypeStruct((nt * out_rows, d_ff), F32)],
        scratch_shapes=[pltpu.VMEM((tm, d), BF16), pltpu.VMEM((nj, 8, tf), F32)],
        compiler_params=_cparams("arbitrary", "arbitrary"),
        name="conv_ffn",
    )(x, g.reshape(1, d), w_up, w_up, conv_w, conv_b.reshape(1, d_ff), w_down, ext)
    return out, a_tail, tiles_per_seq


def _norm_kernel(x_ref, g_ref, o_ref):
    o_ref[...] = _rmsnorm(x_ref[...], g_ref[...])


def _final_norm(x, g):
    n, d = x.shape
    tm = _pick_tile(n, 1040)
    return pl.pallas_call(
        _norm_kernel,
        grid=(n // tm,),
        in_specs=[pl.BlockSpec((tm, d), lambda i: (i, 0)), pl.BlockSpec((1, d), lambda i: (0, 0))],
        out_specs=pl.BlockSpec((tm, d), lambda i: (i, 0)),
        out_shape=jax.ShapeDtypeStruct((n, d), F32),
        compiler_params=_cparams("parallel"),
        name="final_norm",
    )(x, g.reshape(1, d))


def _norm_tail_kernel(x_ref, g_ref, o_ref):
    skip = x_ref.shape[1] - o_ref.shape[1]
    o_ref[0] = _rmsnorm(x_ref[0, skip:, :], g_ref[...])


def _final_norm_tail(x3, g, keep):
    nb, seq, d = x3.shape
    assert (seq - keep) % 8 == 0
    return pl.pallas_call(
        _norm_tail_kernel,
        grid=(nb,),
        in_specs=[pl.BlockSpec((1, seq, d), lambda b: (b, 0, 0)), pl.BlockSpec((1, d), lambda b: (0, 0))],
        out_specs=pl.BlockSpec((1, keep, d), lambda b: (b, 0, 0)),
        out_shape=jax.ShapeDtypeStruct((nb, keep, d), F32),
        compiler_params=_cparams("parallel"),
        name="final_norm_tail",
    )(x3, g.reshape(1, d))


def _reorder_in_proj(w_in, b_in):
    n_hf = 4 * HG_W + 3 * FOX_W
    pad = LANES - FOX_HEADS

    def reorder(a):
        parts = [a[..., n_hf + FOX_HEADS:], a[..., :n_hf], a[..., n_hf:n_hf + FOX_HEADS],
                 jnp.zeros(a.shape[:-1] + (pad,), a.dtype)]
        return jnp.concatenate(parts, axis=-1)

    return reorder(w_in).astype(BF16), reorder(b_in)


def kernel(x_prompt, x_sample, cache_k, cache_v, cache_logf, state_hgrn, state_conv, page_table, meta_tokens, norm1, norm2, norm_f, w_in, b_in, hg_lower_bounds, hg_norm, w_oa, w_ob, w_o, w_up, conv_w, conv_b, w_down):
    nb, s_len, d = x_prompt.shape
    seq_p = s_len + N_META
    db, seq_s, _ = x_sample.shape
    depth = w_in.shape[0]
    d_ff = w_down.shape[1]
    n_pool, page = cache_k.shape[1], cache_k.shape[2]
    assert w_in.shape[2] == 4 * HG_W + 3 * FOX_W + FOX_HEADS + 2 * d and 2 * d == C_ZQ

    w_in_r, b_in_r = _reorder_in_proj(w_in, b_in)
    w_oa, w_ob, w_o, w_up, w_down = (w.astype(BF16) for w in (w_oa, w_ob, w_o, w_up, w_down))
    cache_kt = jnp.transpose(cache_k, (0, 1, 3, 4, 2))
    cache_vt = jnp.transpose(cache_v, (0, 1, 3, 4, 2))
    cache_lft = jnp.swapaxes(cache_logf, 2, 3)
    lbraw = hg_lower_bounds.astype(F32)

    meta = jnp.broadcast_to(meta_tokens.astype(x_prompt.dtype)[None], (nb, N_META, d))
    xp = jnp.concatenate([meta, x_prompt], axis=1).reshape(nb * seq_p, d)
    xs = x_sample.reshape(db * seq_s, d)

    outs = [[] for _ in range(5)]
    kv_p = [jnp.zeros((depth, nb, FOX_W, seq_p), F32) for _ in range(2)]
    kv_s = [jnp.zeros((depth, db * seq_s, FOX_W), F32) for _ in range(2)]
    hg_s = jnp.zeros(state_hgrn.shape, F32)
    for l in range(depth):
        z = _inproj(xp, norm1[l], w_in_r[l], b_in_r[l])
        z3 = z.reshape(nb, seq_p, NZ)
        *kv_p, lf, c = _fox_prep(z3, kv_p, l)
        y_hg, s_new = _hgrn_prompt(z3, lbraw, hg_norm[l], l)
        o_fox = _fox_attn(z3, c)
        xp = _mix(xp, y_hg.reshape(nb * seq_p, HG_W), o_fox.reshape(nb * seq_p, FOX_W), z, w_oa[l], w_ob[l], w_o[l])
        xp, a_tail, tps = _ffn(xp, norm2[l], w_up[l], conv_w[l], conv_b[l], w_down[l], None, seq_p)
        outs[0].append(lf[:, :, :FOX_HEADS])
        outs[1].append(s_new)
        outs[2].append(a_tail.reshape(nb, tps, 8, d_ff)[:, tps - 1, 8 - (CONV_W - 1):])
        z, *kv_s = _inproj(xs, norm1[l], w_in_r[l], b_in_r[l], kv_s, l)
        y_hg, hg_s = _hgrn_sample(z, state_hgrn, lbraw, hg_norm[l], l, seq_s, hg_s)
        o_fox, lf = _fox_decode(z, cache_kt, cache_vt, cache_lft, page_table, l, seq_s)
        xs = _mix(xs, y_hg, o_fox, z, w_oa[l], w_ob[l], w_o[l])
        ext = jnp.pad(state_conv[l], ((0, 0), (0, seq_s - (CONV_W - 1)), (0, 0))).reshape(db * seq_s, d_ff)
        xs, a_all, _ = _ffn(xs, norm2[l], w_up[l], conv_w[l], conv_b[l], w_down[l], ext, seq_s)
        outs[3].append(lf.reshape(db, seq_s, LANES)[:, :, :FOX_HEADS])
        outs[4].append(a_all.reshape(db, seq_s, d_ff)[:, seq_s - (CONV_W - 1):])
    y_prompt = _final_norm_tail(xp.reshape(nb, seq_p, d), norm_f, s_len)
    y_sample = _final_norm(xs, norm_f).reshape(db, seq_s, d)
    lf_p, hg_p, cv_p, lf_s, cv_s = (jnp.stack(o) for o in outs)
    k_p, v_p = (jnp.transpose(a.reshape(depth, nb, FOX_HEADS, FOX_DH, seq_p), (0, 1, 4, 2, 3)) for a in kv_p)
    k_s, v_s = (a.reshape(depth, db, seq_s, FOX_HEADS, FOX_DH) for a in kv_s)
    return (y_prompt, y_sample, k_p, v_p, lf_p, hg_p, cv_p, k_s, v_s, lf_s, hg_s, cv_s)
```

```python
import functools

import jax
import jax.numpy as jnp
from jax import lax
from jax.experimental import pallas as pl
from jax.experimental.pallas import tpu as pltpu

F32 = jnp.float32
BF16 = jnp.bfloat16
EPS = 1e-6
NEG_INF = float("-inf")
LOG2E = 1.4426950408889634

N_META = 16
HG_HEADS = 4
HG_DK = 128
FOX_HEADS = 8
FOX_DH = 64
CONV_W = 3

LANES = 128
HG_EXACT = 8
HG_GROUP = 4
FOX_PRUNE = 140.0
VMEM_LIMIT = 56 * 1024 * 1024

HG_W = HG_HEADS * HG_DK
FOX_W = FOX_HEADS * FOX_DH
C_GA = 0
C_GB = 1024
C_ZQ = 2048
C_ZF = C_ZQ + HG_W
C_ZI = C_ZF + HG_W
C_ZG = C_ZI + HG_W
C_FQ = C_ZG + HG_W
C_FK = C_FQ + FOX_W
C_FV = C_FK + FOX_W
C_FF = C_FV + FOX_W
NZ = C_FF + LANES


def _cparams(*sem):
    return pltpu.CompilerParams(dimension_semantics=sem, vmem_limit_bytes=VMEM_LIMIT)


def _pick_tile(n, cap):
    best = None
    for d in range(8, min(n, cap) + 1, 8):
        if n % d == 0:
            best = d
    assert best is not None, (n, cap)
    return best


def _sigmoid(x):
    return 0.5 * jnp.tanh(0.5 * x) + 0.5


def _log_sigmoid(x):
    return jnp.minimum(x, 0.0) - jnp.log1p(jnp.exp(-jnp.abs(x)))


def _split3(x):
    hi = x.astype(BF16)
    r = x - hi.astype(F32)
    mid = r.astype(BF16)
    r = r - mid.astype(F32)
    return hi, mid, r.astype(BF16)


def _dot_f32(a, b, dims=None):
    if dims is None:
        return jnp.dot(a, b, preferred_element_type=F32)
    return lax.dot_general(a, b, (dims, ((), ())), preferred_element_type=F32)


NT = ((1,), (1,))


def _mask_dot_left(m, x):
    return sum(_dot_f32(m, p) for p in _split3(x))


def _mask_dot_right(x, m):
    return sum(_dot_f32(p, m) for p in _split3(x))


def _rmsnorm(x, g):
    return x * lax.rsqrt(jnp.mean(x * x, axis=-1, keepdims=True) + EPS) * g


def _iota(shape, dim):
    return lax.broadcasted_iota(jnp.int32, shape, dim)


def _inproj_kernel(x_ref, g_ref, w_ref, b_ref, *rest, tn, with_kv):
    if with_kv:
        z_ref, k_ref, v_ref, h_scr = rest[2:]
    else:
        z_ref, h_scr = rest
    j = pl.program_id(1)

    @pl.when(j == 0)
    def _():
        h_scr[...] = _rmsnorm(x_ref[...], g_ref[...]).astype(BF16)

    z = _dot_f32(h_scr[...], w_ref[...]) + b_ref[...]
    z_ref[...] = z

    if with_kv:
        @pl.when(j == C_FK // tn)
        def _():
            k_ref[...] = z[:, C_FK % tn:C_FK % tn + FOX_W]
            v_ref[...] = z[:, C_FV % tn:C_FV % tn + FOX_W]


def _inproj(x, g, w, b, kv=None, layer=0):
    n, d = x.shape
    tm = _pick_tile(n, 2064)
    tn = 1152
    assert NZ % tn == 0 and C_FK // tn == (C_FV + FOX_W - 1) // tn
    in_specs = [
        pl.BlockSpec((tm, d), lambda i, j: (i, 0)),
        pl.BlockSpec((1, d), lambda i, j: (0, 0)),
        pl.BlockSpec((d, tn), lambda i, j: (0, j)),
        pl.BlockSpec((1, tn), lambda i, j: (0, j)),
    ]
    out_specs = [pl.BlockSpec((tm, tn), lambda i, j: (i, j))]
    out_shape = [jax.ShapeDtypeStruct((n, NZ), F32)]
    args = [x, g.reshape(1, d), w, b.reshape(1, NZ)]
    aliases = {}
    if kv is not None:
        in_specs += [pl.BlockSpec(memory_space=pl.ANY)] * 2
        out_specs += [pl.BlockSpec((None, tm, FOX_W), lambda i, j: (layer, i, 0))] * 2
        out_shape += [jax.ShapeDtypeStruct(a.shape, F32) for a in kv]
        args += list(kv)
        aliases = {4: 1, 5: 2}
    out = pl.pallas_call(
        functools.partial(_inproj_kernel, tn=tn, with_kv=kv is not None),
        grid=(n // tm, NZ // tn),
        in_specs=in_specs,
        out_specs=out_specs,
        out_shape=out_shape,
        input_output_aliases=aliases,
        scratch_shapes=[pltpu.VMEM((tm, d), BF16)],
        compiler_params=_cparams("parallel", "arbitrary"),
        name="inproj",
    )(*args)
    return out if kv is not None else out[0]


def _hgrn_lower_bound(lbraw, layer):
    e = jnp.exp(lbraw - jnp.max(lbraw, axis=0, keepdims=True))
    sm = e / jnp.sum(e, axis=0, keepdims=True)
    lb = jnp.zeros_like(sm[0:1])
    for i in range(1, layer + 1):
        lb = lb + sm[i:i + 1]
    return lb


def _hgrn_gates(zq, zf, zi, lb):
    q = zq * _sigmoid(zq)
    a = jnp.log(lb)
    b = jnp.log1p(-lb) + _log_sigmoid(zf)
    logf = jnp.maximum(a, b) + jnp.log1p(jnp.exp(-jnp.abs(a - b)))
    k = (1.0 - lb) * _sigmoid(-zf)
    return q, k, zi, logf


def _hgrn_diag(q, k, b, v):
    n = q.shape[0]
    rowid = _iota(q.shape, 0)
    acc = jnp.zeros_like(v)
    for s in range(n):
        d = jnp.where(rowid >= s, b - b[s:s + 1], NEG_INF)
        w = q * k[s:s + 1] * jnp.exp2(d)
        acc = acc + jnp.sum(w, axis=-1, keepdims=True) * v[s:s + 1]
    return acc


def _hgrn_head_out(o, zg, gain):
    o = o * lax.rsqrt(jnp.mean(o * o, axis=-1, keepdims=True) + EPS)
    return o * gain * (zg * _sigmoid(zg))


def _pad_rows(x, rows):
    if x.shape[0] == rows:
        return x
    return jnp.concatenate([x, jnp.zeros((rows - x.shape[0], x.shape[1]), x.dtype)], axis=0)


def _hgrn_prompt_kernel(zq_ref, zf_ref, zi_ref, zg_ref, lbraw_ref, gain_ref, y_ref, s_ref, st_scr,
                        *, layer, seq):
    ck = LANES
    n_full, tail = divmod(seq, ck)
    assert tail % HG_EXACT == 0
    lb_all = _hgrn_lower_bound(lbraw_ref[...], layer)
    gain_all = gain_ref[...]
    row = _iota((ck, ck), 0)
    col = _iota((ck, ck), 1)
    ltri = jnp.where(col <= row, 1.0, 0.0).astype(BF16)
    levels = []
    half = HG_EXACT
    while half < ck:
        levels.append((half, row % (2 * half) >= half, row // (2 * half) == col // (2 * half)))
        half *= 2
    st_scr[...] = jnp.zeros_like(st_scr)

    def chunk(r0, n_rows):
        for g in range(HG_GROUP):
            head_chunk(r0, n_rows, g, slice(HG_DK * g, HG_DK * (g + 1)))

    def head_chunk(r0, n_rows, g, hs):
        sl = pl.ds(r0, n_rows)
        lb, gain = lb_all[:, hs], gain_all[:, hs]
        q, k, v, logf = _hgrn_gates(zq_ref[0, sl, hs], zf_ref[0, sl, hs], zi_ref[0, sl, hs], lb)
        q, k, v, logf = (_pad_rows(a, ck) for a in (q, k, v, logf))
        b = _mask_dot_left(ltri, logf * LOG2E)
        bend = b[ck - 1:ck]
        a = jnp.zeros((ck, ck), F32)
        for half, upper, same_blk in levels:
            blk = 2 * half
            bmid = jnp.concatenate(
                [jnp.broadcast_to(b[blk * i + half - 1:blk * i + half], (blk, ck)) for i in range(ck // blk)],
                axis=0)
            q2 = jnp.where(upper, q * jnp.exp2(jnp.minimum(b - bmid, 0.0)), 0.0).astype(BF16)
            k2 = jnp.where(upper, 0.0, k * jnp.exp2(jnp.minimum(bmid - b, 0.0))).astype(BF16)
            a = a + jnp.where(same_blk, _dot_f32(q2, k2, NT), 0.0)
        st = st_scr[g]
        o = _dot_f32(a.astype(BF16), v.astype(BF16))
        o = o + _dot_f32((q * jnp.exp2(b)).astype(BF16), st.astype(BF16), NT)
        diag = [_hgrn_diag(*(x[HG_EXACT * i:HG_EXACT * (i + 1)] for x in (q, k, b, v)))
                for i in range(n_rows // HG_EXACT)]
        o = o[:n_rows] + jnp.concatenate(diag, axis=0)
        st_scr[g] = st * jnp.exp2(bend) + _dot_f32(v.T.astype(BF16), (k * jnp.exp2(bend - b)).astype(BF16))
        y_ref[0, sl, hs] = _hgrn_head_out(o, zg_ref[0, sl, hs], gain).astype(y_ref.dtype)

    def body(c, carry):
        chunk(pl.multiple_of(c * ck, ck), ck)
        return carry

    lax.fori_loop(0, n_full, body, 0)
    if tail:
        chunk(n_full * ck, tail)
    for g in range(HG_GROUP):
        s_ref[0, g] = st_scr[g].T


def _hgrn_prompt(z3, lbraw, gain, layer):
    nb, seq, _ = z3.shape
    gw = HG_GROUP * HG_DK

    def zspec(c0):
        return pl.BlockSpec((1, seq, gw), lambda b, h, c0=c0: (b, 0, c0 // gw + h))

    return pl.pallas_call(
        functools.partial(_hgrn_prompt_kernel, layer=layer, seq=seq),
        grid=(nb, HG_HEADS // HG_GROUP),
        in_specs=[zspec(C_ZQ), zspec(C_ZF), zspec(C_ZI), zspec(C_ZG),
                  pl.BlockSpec((lbraw.shape[0], gw), lambda b, h: (0, h)),
                  pl.BlockSpec((1, gw), lambda b, h: (0, h))],
        out_specs=[pl.BlockSpec((1, seq, gw), lambda b, h: (b, 0, h)),
                   pl.BlockSpec((1, HG_GROUP, HG_DK, HG_DK), lambda b, h: (b, h, 0, 0))],
        out_shape=[jax.ShapeDtypeStruct((nb, seq, HG_W), BF16),
                   jax.ShapeDtypeStruct((nb, HG_HEADS, HG_DK, HG_DK), F32)],
        scratch_shapes=[pltpu.VMEM((HG_GROUP, HG_DK, HG_DK), F32)],
        compiler_params=_cparams("parallel", "parallel"),
        name="hgrn_prompt",
    )(z3, z3, z3, z3, lbraw, gain.reshape(1, HG_W))


def _hgrn_sample_kernel(zq_ref, zf_ref, zi_ref, zg_ref, lbraw_ref, gain_ref, s0_ref, stack_hbm, y_ref, s_ref,
                        *, layer, seq):
    del stack_hbm
    rows = zq_ref.shape[0]
    lb_all = _hgrn_lower_bound(lbraw_ref[...], layer)
    row = _iota((rows, rows), 0)
    col = _iota((rows, rows), 1)
    ltri = jnp.where((row // seq == col // seq) & (col <= row), 1.0, 0.0).astype(BF16)
    for g in range(HG_HEADS):
        hs = slice(HG_DK * g, HG_DK * (g + 1))
        _hgrn_sample_head(zq_ref, zf_ref, zi_ref, zg_ref, gain_ref, s0_ref, y_ref, s_ref, lb_all[:, hs], ltri,
                          g, hs, seq)


def _hgrn_sample_head(zq_ref, zf_ref, zi_ref, zg_ref, gain_ref, s0_ref, y_ref, s_ref, lb, ltri, g, hs, seq):
    rows = zq_ref.shape[0]
    n_seq = rows // seq
    q, k, v, logf = _hgrn_gates(zq_ref[:, hs], zf_ref[:, hs], zi_ref[:, hs], lb)
    b = _mask_dot_left(ltri, logf * LOG2E)
    bend = jnp.concatenate(
        [jnp.broadcast_to(b[seq * s + seq - 1:seq * (s + 1)], (seq, LANES)) for s in range(n_seq)], axis=0)
    qt = (q * jnp.exp2(b)).astype(BF16)
    kt = k * jnp.exp2(bend - b)
    outs = []
    for s in range(n_seq):
        rs = slice(seq * s, seq * (s + 1))
        s0 = s0_ref[s, g]
        outs.append(_dot_f32(qt[rs], s0.astype(BF16)) + _hgrn_diag(q[rs], k[rs], b[rs], v[rs]))
        x = jnp.concatenate([kt[rs], jnp.exp2(bend[rs]), jnp.zeros((HG_DK - 2 * seq, LANES), F32)], axis=0)
        xt = x.T
        inc = _dot_f32(xt.astype(BF16), _pad_rows(v[rs], HG_DK).astype(BF16))
        s_ref[s, g] = xt[:, seq:seq + 1] * s0 + inc
    o = jnp.concatenate(outs, axis=0)
    y_ref[:, hs] = _hgrn_head_out(o, zg_ref[:, hs], gain_ref[:, hs]).astype(y_ref.dtype)


def _hgrn_sample(z, state, lbraw, gain, layer, seq, new_state):
    n = z.shape[0]
    rows = LANES
    assert n % rows == 0 and rows % seq == 0 and 2 * seq <= HG_DK
    sblk = rows // seq

    def zspec(c0):
        return pl.BlockSpec((rows, HG_W), lambda i, c0=c0: (i, c0 // HG_W))

    st_spec = pl.BlockSpec((None, sblk, HG_HEADS, HG_DK, HG_DK), lambda i: (layer, i, 0, 0, 0))
    in_specs = [zspec(C_ZQ), zspec(C_ZF), zspec(C_ZI), zspec(C_ZG),
                pl.BlockSpec(lbraw.shape, lambda i: (0, 0)),
                pl.BlockSpec((1, HG_W), lambda i: (0, 0)),
                st_spec,
                pl.BlockSpec(memory_space=pl.ANY)]
    args = [z, z, z, z, lbraw, gain.reshape(1, HG_W), state, new_state]
    return pl.pallas_call(
        functools.partial(_hgrn_sample_kernel, layer=layer, seq=seq),
        grid=(n // rows,),
        in_specs=in_specs,
        out_specs=[pl.BlockSpec((rows, HG_W), lambda i: (i, 0)), st_spec],
        out_shape=[jax.ShapeDtypeStruct((n, HG_W), BF16),
                   jax.ShapeDtypeStruct(state.shape, F32)],
        input_output_aliases={7: 1},
        compiler_params=_cparams("parallel"),
        name="hgrn_sample",
    )(*args)


def _fox_gate(ff_ref, lf_ref, c_ref, seq):
    ck = LANES
    n_full, tail = divmod(seq, ck)
    ltri = jnp.where(_iota((ck, ck), 1) <= _iota((ck, ck), 0), 1.0, 0.0).astype(BF16)

    def chunk(r0, n_rows, carry):
        sl = pl.ds(r0, n_rows)
        lf = _log_sigmoid(ff_ref[0, sl, :])
        lf_ref[0, sl, :] = lf
        c = _mask_dot_left(ltri, _pad_rows(lf, ck)) + carry
        c_ref[0, sl, :] = c[:n_rows]
        return c[n_rows - 1:n_rows]

    carry = lax.fori_loop(0, n_full, lambda i, cr: chunk(pl.multiple_of(i * ck, ck), ck, cr),
                          jnp.zeros((1, LANES), F32))
    if tail:
        chunk(n_full * ck, tail, carry)


def _fox_attn_kernel(fq_ref, fk_ref, fv_ref, c_ref, o_ref, qa_scr, ka_scr, va_scr, r_scr, acc_scr,
                     qn_scr, kn_scr, *, seq, bq):
    padded = qa_scr.shape[1]
    nq = padded // bq
    ck = LANES
    n_full, tail = divmod(seq, ck)
    scale = FOX_DH ** -0.5 * LOG2E

    lane8 = _iota((8, LANES), 1)

    def max_sq_norm(x, own):
        return jnp.max(jnp.sum(jnp.where(own, x * x, 0.0), axis=-1, keepdims=True), axis=0, keepdims=True)

    def build(i, r0, n_rows):
        sl = pl.ds(r0, n_rows)
        cb = c_ref[0, sl, :]
        lane = _iota((n_rows, LANES), 1)
        qn = jnp.zeros((8, LANES), F32)
        kn = jnp.zeros((8, LANES), F32)
        for h in range(FOX_HEADS):
            hh = h % 2
            ps = slice(LANES * (h // 2), LANES * (h // 2 + 1))
            qf, kf, vf = fq_ref[0, sl, ps], fk_ref[0, sl, ps], fv_ref[0, sl, ps]
            ch = jnp.sum(jnp.where(lane == h, cb, 0.0), axis=-1, keepdims=True) * LOG2E
            hi, mid, lo = (p.astype(F32) for p in _split3(ch))
            own = (lane >= FOX_DH * hh) & (lane < FOX_DH * (hh + 1))
            f0 = FOX_DH * (1 - hh)
            ones_q = (lane >= f0 + 3) & (lane < f0 + 6)
            ones_k = (lane >= f0) & (lane < f0 + 3)
            qa = jnp.where(own, qf * scale,
                           jnp.where(lane == f0, hi, jnp.where(lane == f0 + 1, mid, jnp.where(
                               lane == f0 + 2, lo, jnp.where(ones_q, 1.0, 0.0)))))
            ka = jnp.where(own, kf,
                           jnp.where(ones_k, 1.0, jnp.where(lane == f0 + 3, -hi, jnp.where(
                               lane == f0 + 4, -mid, jnp.where(lane == f0 + 5, -lo, 0.0)))))
            qa_scr[h, sl, :] = qa.astype(BF16)
            ka_scr[h, sl, :] = ka.astype(BF16)
            va_scr[h, sl, :] = jnp.where(own, vf, 1.0).astype(BF16)
            qn = jnp.where(lane8 == h, max_sq_norm(qf, own), qn)
            kn = jnp.where(lane8 == h, max_sq_norm(kf, own), kn)
        qn_scr[i] = qn
        kn_scr[i] = kn

    def build_body(i, carry):
        build(i, pl.multiple_of(i * ck, ck), ck)
        return carry

    lax.fori_loop(0, n_full, build_body, 0)
    if tail:
        build(n_full, n_full * ck, tail)
    kmax = jnp.max(kn_scr[...], axis=0)
    if padded > seq:
        zpad = jnp.zeros((padded - seq, LANES), BF16)
        for h in range(FOX_HEADS):
            qa_scr[h, seq:padded, :] = zpad
            ka_scr[h, seq:padded, :] = zpad
            va_scr[h, seq:padded, :] = zpad

    kc = 2 * bq

    def qblock(q0, n_chunks, tail_k0, tail_w, rows):
        qa = [qa_scr[h, pl.ds(q0, rows), :] for h in range(FOX_HEADS)]
        tail_mask = _iota((rows, tail_w), 1) <= _iota((rows, tail_w), 0) + (tail_w - bq)
        lane_q = _iota((rows, LANES), 1)

        def scores(hh, k0, width, mask):
            s = _dot_f32(qa[hh], ka_scr[hh, pl.ds(k0, width), :], NT)
            return s if mask is None else jnp.where(mask, s, NEG_INF)

        def step(k0, width, mask):
            for hh in range(FOX_HEADS):
                s = scores(hh, k0, width, mask)
                m_old = r_scr[hh, :rows]
                blk = s[:, :LANES]
                for t in range(1, width // LANES):
                    blk = jnp.maximum(blk, s[:, LANES * t:LANES * (t + 1)])
                m_new = jnp.maximum(m_old, jnp.broadcast_to(jnp.max(blk, axis=-1, keepdims=True), (rows, LANES)))
                p = jnp.exp2(s - jnp.concatenate([m_new] * (width // LANES), axis=1))
                acc_scr[hh, :rows] = jnp.exp2(m_old - m_new) * acc_scr[hh, :rows] + _dot_f32(
                    p.astype(BF16), va_scr[hh, pl.ds(k0, width), :])
                r_scr[hh, :rows] = m_new

        r_scr[...] = jnp.full(r_scr.shape, NEG_INF, F32)
        acc_scr[...] = jnp.zeros_like(acc_scr)
        step(tail_k0, tail_w, tail_mask)
        i0 = q0 // ck
        qmax = qn_scr[i0]
        for t in range(1, -(-rows // ck)):
            qmax = jnp.maximum(qmax, qn_scr[i0 + t])
        mmin = jnp.zeros((8, LANES), F32)
        for h in range(FOX_HEADS):
            mmin = jnp.where(lane8 == h, jnp.min(r_scr[h, :rows], axis=0, keepdims=True), mmin)
        thr = (mmin - FOX_PRUNE - jnp.sqrt(qmax * kmax) * (scale * 1.05))[0:1]
        c_q = c_ref[0, pl.ds(q0, 1), :]

        def body(jj, carry):
            k0 = pl.multiple_of((n_chunks - 1 - jj) * kc, kc)
            c_k = c_ref[0, pl.ds(k0 + kc - 1, 1), :]
            reach = jnp.where(lane8[0:1] < FOX_HEADS, (c_q - c_k) * LOG2E - thr, NEG_INF)

            @pl.when(jnp.max(reach) >= 0.0)
            def _():
                step(k0, kc, None)
            return carry

        lax.fori_loop(0, n_chunks, body, 0)
        outs = [acc_scr[h, :rows] / pltpu.roll(acc_scr[h, :rows], FOX_DH, 1) for h in range(FOX_HEADS)]
        for pr in range(FOX_HEADS // 2):
            o = jnp.where(lane_q < FOX_DH, outs[2 * pr], outs[2 * pr + 1])
            o_ref[0, pl.ds(q0, rows), LANES * pr:LANES * (pr + 1)] = o.astype(o_ref.dtype)

    def even_block(u, rows=bq):
        q0 = pl.multiple_of(u * kc, kc) if not isinstance(u, int) else u * kc
        qblock(q0, u, q0, bq, rows)

    def odd_block(u, rows=bq):
        k0 = pl.multiple_of(u * kc, kc) if not isinstance(u, int) else u * kc
        qblock(k0 + bq, u, k0, kc, rows)

    def pair_body(u, carry):
        even_block(u)
        odd_block(u)
        return carry

    n_whole = seq // bq
    lax.fori_loop(0, n_whole // 2, pair_body, 0)
    rest = [(i, bq) for i in range(n_whole - n_whole % 2, n_whole)]
    if n_whole < nq:
        rest.append((n_whole, seq - n_whole * bq))
    for i, rows in rest:
        (odd_block if i % 2 else even_block)(i // 2, rows)


def _fox_attn(z3, c):
    nb, seq, _ = z3.shape
    bq = 256
    padded = -(-seq // bq) * bq

    def zspec(c0):
        return pl.BlockSpec((1, seq, FOX_W), lambda b, c0=c0: (b, 0, c0 // FOX_W))

    return pl.pallas_call(
        functools.partial(_fox_attn_kernel, seq=seq, bq=bq),
        grid=(nb,),
        in_specs=[zspec(C_FQ), zspec(C_FK), zspec(C_FV),
                  pl.BlockSpec((1, seq, LANES), lambda b: (b, 0, 0))],
        out_specs=pl.BlockSpec((1, seq, FOX_W), lambda b: (b, 0, 0)),
        out_shape=jax.ShapeDtypeStruct((nb, seq, FOX_W), BF16),
        scratch_shapes=[pltpu.VMEM((FOX_HEADS, padded, LANES), BF16)] * 3
        + [pltpu.VMEM((FOX_HEADS, bq, LANES), F32)] * 2
        + [pltpu.VMEM((-(-seq // LANES), 8, LANES), F32)] * 2,
        compiler_params=_cparams("parallel"),
        name="fox_attn",
    )(z3, z3, z3, c)


def _fox_prep_kernel(fk_ref, fv_ref, ff_ref, k_hbm, v_hbm, kt_ref, vt_ref, lf_ref, c_ref, tail_scr, *, seq):
    del k_hbm, v_hbm
    _fox_gate(ff_ref, lf_ref, c_ref, seq)
    n_full, tail = divmod(seq, LANES)
    for src, dst in ((fk_ref, kt_ref), (fv_ref, vt_ref)):
        for i in range(n_full):
            dst[:, LANES * i:LANES * (i + 1)] = src[0, LANES * i:LANES * (i + 1), :].T
        if tail:
            tail_scr[...] = _pad_rows(src[0, n_full * LANES:seq, :], LANES).T
            dst[:, n_full * LANES:seq] = tail_scr[:, :tail]


def _fox_prep(z3, stacks, layer):
    nb, seq, _ = z3.shape
    kv_spec = pl.BlockSpec((None, None, FOX_W, seq), lambda b: (layer, b, 0, 0))
    g_spec = pl.BlockSpec((1, seq, LANES), lambda b: (b, 0, 0))
    in_specs = [pl.BlockSpec((1, seq, FOX_W), lambda b: (b, 0, C_FK // FOX_W)),
                pl.BlockSpec((1, seq, FOX_W), lambda b: (b, 0, C_FV // FOX_W)),
                pl.BlockSpec((1, seq, LANES), lambda b: (b, 0, C_FF // LANES)),
                pl.BlockSpec(memory_space=pl.ANY), pl.BlockSpec(memory_space=pl.ANY)]
    kv_shape = jax.ShapeDtypeStruct(stacks[0].shape, F32)
    g_shape = jax.ShapeDtypeStruct((nb, seq, LANES), F32)
    return pl.pallas_call(
        functools.partial(_fox_prep_kernel, seq=seq),
        grid=(nb,),
        in_specs=in_specs,
        out_specs=[kv_spec, kv_spec, g_spec, g_spec],
        out_shape=[kv_shape, kv_shape, g_shape, g_shape],
        input_output_aliases={3: 0, 4: 1},
        scratch_shapes=[pltpu.VMEM((FOX_W, LANES), F32)],
        compiler_params=_cparams("parallel"),
        name="fox_prep",
    )(z3, z3, z3, *stacks)


def _fox_decode_kernel(pt_ref, fq_ref, fk_ref, fv_ref, ff_ref, *rest, n_pages, page, seq):
    del pt_ref
    k_refs, v_refs, lf_refs = rest[:n_pages], rest[n_pages:2 * n_pages], rest[2 * n_pages:3 * n_pages]
    o_ref, lfo_ref = rest[3 * n_pages:]
    assert page == LANES
    rows = FOX_HEADS * seq
    lfn = _log_sigmoid(ff_ref[...])
    lfo_ref[...] = lfn
    lfn = jnp.where(_iota(lfn.shape, 1) < FOX_HEADS, lfn, 0.0)
    lfn_t = _pad_rows(lfn, LANES).T[:FOX_HEADS]
    stack = jnp.concatenate([r[0, 0] for r in lf_refs] + [lfn_t], axis=0)
    ustrict = jnp.where(_iota((LANES, LANES), 0) > _iota((LANES, LANES), 1), 1.0, 0.0).astype(BF16)
    within = _mask_dot_right(stack, ustrict)
    tot = jnp.sum(stack, axis=-1, keepdims=True)
    g = [None] * (n_pages + 1)
    run = jnp.zeros((FOX_HEADS, 1), F32)
    for p in reversed(range(n_pages + 1)):
        rs = slice(FOX_HEADS * p, FOX_HEADS * (p + 1))
        g[p] = within[rs] + run
        run = run + tot[rs]
    gq = _pad_rows(g[n_pages], LANES).T[:seq]
    lane = _iota((seq, LANES), 1)
    gq_col = jnp.concatenate(
        [jnp.sum(jnp.where(lane == h, gq, 0.0), axis=-1, keepdims=True) for h in range(FOX_HEADS)], axis=0)

    def bias(gp):
        return jnp.concatenate(
            [jnp.broadcast_to(gp[h:h + 1], (seq, page)) for h in range(FOX_HEADS)], axis=0) - gq_col

    own = _iota((rows, FOX_W), 0) // seq == _iota((rows, FOX_W), 1) // FOX_DH
    q = fq_ref[...] * (FOX_DH ** -0.5)
    q_bd = jnp.where(own, jnp.concatenate([q] * FOX_HEADS, axis=0), 0.0).astype(BF16)
    s_blocks = []
    for p in range(n_pages):
        kt = k_refs[p][0, 0].reshape(FOX_W, page).astype(BF16)
        s_blocks.append(_dot_f32(q_bd, kt) + bias(g[p]))
    s = _dot_f32(q_bd, _pad_rows(fk_ref[...], page).astype(BF16), NT) + bias(g[n_pages])
    s_blocks.append(jnp.where(_iota((rows, page), 1) <= _iota((rows, page), 0) % seq, s, NEG_INF))
    mx = s_blocks[0]
    for s in s_blocks[1:]:
        mx = jnp.maximum(mx, s)
    m = jnp.max(mx, axis=-1, keepdims=True)
    wsum = jnp.zeros((rows, page), F32)
    acc = jnp.zeros((rows, FOX_W), F32)
    for p in range(n_pages + 1):
        pe = jnp.exp(s_blocks[p] - m)
        wsum = wsum + pe
        if p < n_pages:
            acc = acc + _dot_f32(pe.astype(BF16), v_refs[p][0, 0].reshape(FOX_W, page).astype(BF16), NT)
        else:
            acc = acc + _dot_f32(pe.astype(BF16), _pad_rows(fv_ref[...], page).astype(BF16))
    acc = jnp.where(own, acc / jnp.sum(wsum, axis=-1, keepdims=True), 0.0)
    o = acc[:seq]
    for h in range(1, FOX_HEADS):
        o = o + acc[seq * h:seq * (h + 1)]
    o_ref[...] = o


def _fox_decode(z, cache_kt, cache_vt, cache_lft, page_table, layer, seq):
    n = z.shape[0]
    nb, n_pages = page_table.shape
    page = cache_lft.shape[-1]
    assert nb * seq == n and seq == 8

    def zspec(c0, w):
        return pl.BlockSpec((seq, w), lambda b, pt, c0=c0, w=w: (b, c0 // w))

    def pspec(shape, p):
        return pl.BlockSpec(shape, lambda b, pt, p=p: (layer, pt[b, p]) + (0,) * (len(shape) - 2))

    kv_shape = (1, 1, FOX_HEADS, FOX_DH, page)
    lf_shape = (1, 1, FOX_HEADS, page)
    in_specs = [zspec(C_FQ, FOX_W), zspec(C_FK, FOX_W), zspec(C_FV, FOX_W), zspec(C_FF, LANES)]
    in_specs += [pspec(kv_shape, p) for p in range(n_pages)]
    in_specs += [pspec(kv_shape, p) for p in range(n_pages)]
    in_specs += [pspec(lf_shape, p) for p in range(n_pages)]
    grid_spec = pltpu.PrefetchScalarGridSpec(
        num_scalar_prefetch=1,
        grid=(nb,),
        in_specs=in_specs,
        out_specs=[pl.BlockSpec((seq, FOX_W), lambda b, pt: (b, 0)),
                   pl.BlockSpec((seq, LANES), lambda b, pt: (b, 0))],
    )
    return pl.pallas_call(
        functools.partial(_fox_decode_kernel, n_pages=n_pages, page=page, seq=seq),
        grid_spec=grid_spec,
        out_shape=[jax.ShapeDtypeStruct((n, FOX_W), F32),
                   jax.ShapeDtypeStruct((n, LANES), F32)],
        compiler_params=_cparams("arbitrary"),
        name="fox_decode",
    )(page_table, z, z, z, z, *([cache_kt] * n_pages), *([cache_vt] * n_pages), *([cache_lft] * n_pages))


def _mix_kernel(x_ref, ya_ref, yb_ref, ga_ref, gb_ref, woa_ref, wob_ref, wo_ref, o_ref):
    ya = _dot_f32(ya_ref[...].astype(BF16), woa_ref[...])
    yb = _dot_f32(yb_ref[...].astype(BF16), wob_ref[...])
    merged = _sigmoid(ga_ref[...]) * ya + _sigmoid(gb_ref[...]) * yb
    o_ref[...] = x_ref[...] + _dot_f32(merged.astype(BF16), wo_ref[...])


def _mix(x, ya, yb, z, woa, wob, wo):
    n, d = x.shape
    assert d == C_GB - C_GA
    tm = _pick_tile(n, 704)
    row = lambda w: pl.BlockSpec((tm, w), lambda i: (i, 0))
    full = lambda a: pl.BlockSpec(a.shape, lambda i: (0, 0))
    return pl.pallas_call(
        _mix_kernel,
        grid=(n // tm,),
        in_specs=[row(d), row(HG_W), row(FOX_W),
                  pl.BlockSpec((tm, d), lambda i: (i, C_GA // d)),
                  pl.BlockSpec((tm, d), lambda i: (i, C_GB // d)),
                  full(woa), full(wob), full(wo)],
        out_specs=row(d),
        out_shape=jax.ShapeDtypeStruct((n, d), F32),
        compiler_params=_cparams("parallel"),
        name="mix_out",
    )(x, ya, yb, z, z, woa, wob, wo)


def _ffn_kernel(x_ref, g_ref, wa_ref, wg_ref, cw_ref, cb_ref, wd_ref, ext_ref, o_ref, at_ref,
                h_scr, carry_scr, *, seq, tiles_per_seq):
    i, j = pl.program_id(0), pl.program_id(1)
    tm = x_ref.shape[0]

    @pl.when(j == 0)
    def _():
        x = x_ref[...]
        h_scr[...] = _rmsnorm(x, g_ref[...]).astype(BF16)
        o_ref[...] = x

    h = h_scr[...]
    a = _dot_f32(h, wa_ref[...])
    gate = _dot_f32(h, wg_ref[...])
    row = _iota(a.shape, 0)
    r1 = pltpu.roll(a, 1, 0)
    r2 = pltpu.roll(a, 2, 0)
    if tiles_per_seq:
        @pl.when((i == 0) & (j == 0))
        def _():
            carry_scr[...] = jnp.zeros_like(carry_scr)

        prev = carry_scr[j]
        seq_start = i % tiles_per_seq == 0
        p1 = jnp.where(seq_start, 0.0, prev[7:8])
        p2 = jnp.where(seq_start, 0.0, prev[6:7])
        a1 = jnp.where(row == 0, p1, r1)
        a2 = jnp.where(row == 0, p2, jnp.where(row == 1, p1, r2))
        carry_scr[j] = a[tm - 8:tm]
    else:
        ext = ext_ref[...]
        pos = row % seq
        a1 = jnp.where(pos == 0, pltpu.roll(ext, tm - 1, 0), r1)
        a2 = jnp.where(pos < 2, ext, r2)
    cw = cw_ref[...]
    conv = cb_ref[...] + cw[0:1] * a2 + cw[1:2] * a1 + cw[2:3] * a
    act = conv * _sigmoid(conv) * gate
    o_ref[...] += _dot_f32(act.astype(BF16), wd_ref[...])
    at_ref[...] = a[tm - at_ref.shape[0]:]


def _ffn(x, g, w_up, conv_w, conv_b, w_down, ext, seq):
    n, d = x.shape
    d_ff = w_down.shape[0]
    tf = 256
    assert d_ff % tf == 0
    nj = d_ff // tf
    if ext is None:
        tm = _pick_tile(seq, 2064)
        tiles_per_seq = seq // tm
        out_rows = 8
        ext = jnp.zeros((8, d_ff), F32)
        ext_spec = pl.BlockSpec((8, tf), lambda i, j: (0, j))
    else:
        tm = _pick_tile(n, 1040)
        assert tm % seq == 0 and seq == 8
        tiles_per_seq = 0
        out_rows = tm
        ext_spec = pl.BlockSpec((tm, tf), lambda i, j: (i, j))
    nt = n // tm
    out, a_tail = pl.pallas_call(
        functools.partial(_ffn_kernel, seq=seq, tiles_per_seq=tiles_per_seq),
        grid=(nt, nj),
        in_specs=[
            pl.BlockSpec((tm, d), lambda i, j: (i, 0)),
            pl.BlockSpec((1, d), lambda i, j: (0, 0)),
            pl.BlockSpec((d, tf), lambda i, j: (0, j)),
            pl.BlockSpec((d, tf), lambda i, j: (0, j + nj)),
            pl.BlockSpec((CONV_W, tf), lambda i, j: (0, j)),
            pl.BlockSpec((1, tf), lambda i, j: (0, j)),
            pl.BlockSpec((tf, d), lambda i, j: (j, 0)),
            ext_spec,
        ],
        out_specs=[pl.BlockSpec((tm, d), lambda i, j: (i, 0)),
                   pl.BlockSpec((out_rows, tf), lambda i, j: (i, j))],
        out_shape=[jax.ShapeDtypeStruct((n, d), F32),
                   jax.ShapeDtypeStruct((nt * out_rows, d_ff), F32)],
        scratch_shapes=[pltpu.VMEM((tm, d), BF16), pltpu.VMEM((nj, 8, tf), F32)],
        compiler_params=_cparams("arbitrary", "arbitrary"),
        name="conv_ffn",
    )(x, g.reshape(1, d), w_up, w_up, conv_w, conv_b.reshape(1, d_ff), w_down, ext)
    return out, a_tail, tiles_per_seq


def _norm_kernel(x_ref, g_ref, o_ref):
    o_ref[...] = _rmsnorm(x_ref[...], g_ref[...])


def _final_norm(x, g):
    n, d = x.shape
    tm = _pick_tile(n, 1040)
    return pl.pallas_call(
        _norm_kernel,
        grid=(n // tm,),
        in_specs=[pl.BlockSpec((tm, d), lambda i: (i, 0)), pl.BlockSpec((1, d), lambda i: (0, 0))],
        out_specs=pl.BlockSpec((tm, d), lambda i: (i, 0)),
        out_shape=jax.ShapeDtypeStruct((n, d), F32),
        compiler_params=_cparams("parallel"),
        name="final_norm",
    )(x, g.reshape(1, d))


def _norm_tail_kernel(x_ref, g_ref, o_ref):
    skip = x_ref.shape[1] - o_ref.shape[1]
    o_ref[0] = _rmsnorm(x_ref[0, skip:, :], g_ref[...])


def _final_norm_tail(x3, g, keep):
    nb, seq, d = x3.shape
    assert (seq - keep) % 8 == 0
    return pl.pallas_call(
        _norm_tail_kernel,
        grid=(nb,),
        in_specs=[pl.BlockSpec((1, seq, d), lambda b: (b, 0, 0)), pl.BlockSpec((1, d), lambda b: (0, 0))],
        out_specs=pl.BlockSpec((1, keep, d), lambda b: (b, 0, 0)),
        out_shape=jax.ShapeDtypeStruct((nb, keep, d), F32),
        compiler_params=_cparams("parallel"),
        name="final_norm_tail",
    )(x3, g.reshape(1, d))


def _reorder_in_proj(w_in, b_in):
    n_hf = 4 * HG_W + 3 * FOX_W
    pad = LANES - FOX_HEADS

    def reorder(a):
        parts = [a[..., n_hf + FOX_HEADS:], a[..., :n_hf], a[..., n_hf:n_hf + FOX_HEADS],
                 jnp.zeros(a.shape[:-1] + (pad,), a.dtype)]
        return jnp.concatenate(parts, axis=-1)

    return reorder(w_in).astype(BF16), reorder(b_in)


def kernel(x_prompt, x_sample, cache_k, cache_v, cache_logf, state_hgrn, state_conv, page_table, meta_tokens, norm1, norm2, norm_f, w_in, b_in, hg_lower_bounds, hg_norm, w_oa, w_ob, w_o, w_up, conv_w, conv_b, w_down):
    nb, s_len, d = x_prompt.shape
    seq_p = s_len + N_META
    db, seq_s, _ = x_sample.shape
    depth = w_in.shape[0]
    d_ff = w_down.shape[1]
    assert w_in.shape[2] == 4 * HG_W + 3 * FOX_W + FOX_HEADS + 2 * d and 2 * d == C_ZQ

    w_in_r, b_in_r = _reorder_in_proj(w_in, b_in)
    w_oa, w_ob, w_o, w_up, w_down = (w.astype(BF16) for w in (w_oa, w_ob, w_o, w_up, w_down))
    cache_kt = jnp.transpose(cache_k, (0, 1, 3, 4, 2))
    cache_vt = jnp.transpose(cache_v, (0, 1, 3, 4, 2))
    cache_lft = jnp.swapaxes(cache_logf, 2, 3)
    lbraw = hg_lower_bounds.astype(F32)

    meta = jnp.broadcast_to(meta_tokens.astype(x_prompt.dtype)[None], (nb, N_META, d))
    xp = jnp.concatenate([meta, x_prompt], axis=1).reshape(nb * seq_p, d)
    xs = x_sample.reshape(db * seq_s, d)

    outs = [[] for _ in range(5)]
    kv_p = [jnp.zeros((depth, nb, FOX_W, seq_p), F32) for _ in range(2)]
    kv_s = [jnp.zeros((depth, db * seq_s, FOX_W), F32) for _ in range(2)]
    hg_s = jnp.zeros(state_hgrn.shape, F32)
    for l in range(depth):
        z = _inproj(xp, norm1[l], w_in_r[l], b_in_r[l])
        z3 = z.reshape(nb, seq_p, NZ)
        *kv_p, lf, c = _fox_prep(z3, kv_p, l)
        y_hg, s_new = _hgrn_prompt(z3, lbraw, hg_norm[l], l)
        o_fox = _fox_attn(z3, c)
        xp = _mix(xp, y_hg.reshape(nb * seq_p, HG_W), o_fox.reshape(nb * seq_p, FOX_W), z, w_oa[l], w_ob[l], w_o[l])
        xp, a_tail, tps = _ffn(xp, norm2[l], w_up[l], conv_w[l], conv_b[l], w_down[l], None, seq_p)
        outs[0].append(lf[:, :, :FOX_HEADS])
        outs[1].append(s_new)
        outs[2].append(a_tail.reshape(nb, tps, 8, d_ff)[:, tps - 1, 8 - (CONV_W - 1):])
        z, *kv_s = _inproj(xs, norm1[l], w_in_r[l], b_in_r[l], kv_s, l)
        y_hg, hg_s = _hgrn_sample(z, state_hgrn, lbraw, hg_norm[l], l, seq_s, hg_s)
        o_fox, lf = _fox_decode(z, cache_kt, cache_vt, cache_lft, page_table, l, seq_s)
        xs = _mix(xs, y_hg, o_fox, z, w_oa[l], w_ob[l], w_o[l])
        ext = jnp.pad(state_conv[l], ((0, 0), (0, seq_s - (CONV_W - 1)), (0, 0))).reshape(db * seq_s, d_ff)
        xs, a_all, _ = _ffn(xs, norm2[l], w_up[l], conv_w[l], conv_b[l], w_down[l], ext, seq_s)
        outs[3].append(lf.reshape(db, seq_s, LANES)[:, :, :FOX_HEADS])
        outs[4].append(a_all.reshape(db, seq_s, d_ff)[:, seq_s - (CONV_W - 1):])
    y_prompt = _final_norm_tail(xp.reshape(nb, seq_p, d), norm_f, s_len)
    y_sample = _final_norm(xs, norm_f).reshape(db, seq_s, d)
    lf_p, hg_p, cv_p, lf_s, cv_s = (jnp.stack(o) for o in outs)
    k_p, v_p = (jnp.transpose(a.reshape(depth, nb, FOX_HEADS, FOX_DH, seq_p), (0, 1, 4, 2, 3)) for a in kv_p)
    k_s, v_s = (a.reshape(depth, db, seq_s, FOX_HEADS, FOX_DH) for a in kv_s)
    return (y_prompt, y_sample, k_p, v_p, lf_p, hg_p, cv_p, k_s, v_s, lf_s, hg_s, cv_s)
```
